```python
import math
import jax
import jax.numpy as jnp
from jax import lax
import numpy as np


D_MODEL = 1024
BATCH = 16
SEQ = 2048
DEPTH = 1

CHUNK = 64
Q_BLOCK = 128
D_RNN = D_MODEL
RNN_BLOCKS = 16
RNN_BLOCK_W = D_RNN // RNN_BLOCKS
CONV_W = 4
LRU_C = 8.0
HEAD_DIM = 64
N_HEADS = D_MODEL // (2 * HEAD_DIM)
ATTN_W = N_HEADS * 2 * HEAD_DIM
D_FF = 4 * D_MODEL
ROPE_THETA = 10000.0
NORM_EPS = 1e-6
SUBLN_EPS = 1e-5
MASK_VALUE = -1e30
N_IN = 2 * D_RNN + 3 * ATTN_W + 2 * D_MODEL

kernel_name = "hybrid_rglru_diffattn_block"


def rms_norm(x, g, eps=NORM_EPS):
    x32 = x.astype(jnp.float32)
    y = x32 * lax.rsqrt(jnp.mean(x32 * x32, axis=-1, keepdims=True) + eps)
    return (y * g.astype(jnp.float32)).astype(x.dtype)


def rope(x):
    s, d = x.shape[1], x.shape[-1]
    half = d // 2
    inv_freq = ROPE_THETA ** (-jnp.arange(half, dtype=jnp.float32) * 2.0 / d)
    ang = jnp.arange(s, dtype=jnp.float32)[:, None] * inv_freq[None, :]
    cos = jnp.cos(ang)[None, :, None, :]
    sin = jnp.sin(ang)[None, :, None, :]
    x32 = x.astype(jnp.float32)
    x1, x2 = x32[..., :half], x32[..., half:]
    return jnp.concatenate([x1 * cos - x2 * sin, x2 * cos + x1 * sin], axis=-1).astype(x.dtype)


def causal_depthwise_conv(x, w, b):
    s = x.shape[1]
    xp = jnp.pad(x, ((0, 0), (CONV_W - 1, 0), (0, 0)))
    y = xp[:, 0:s] * w[0]
    for j in range(1, CONV_W):
        y = y + xp[:, j:j + s] * w[j]
    return y + b


def rg_lru(x, w_a, b_a, w_x, b_x, lam):
    bsz, s, _ = x.shape
    xb = x.reshape(bsz, s, RNN_BLOCKS, RNN_BLOCK_W)
    r = jax.nn.sigmoid((jnp.einsum('bsni,nij->bsnj', xb, w_a) + b_a).astype(jnp.float32)).reshape(bsz, s, D_RNN)
    i = jax.nn.sigmoid((jnp.einsum('bsni,nij->bsnj', xb, w_x) + b_x).astype(jnp.float32)).reshape(bsz, s, D_RNN)
    log_a = -LRU_C * r * jax.nn.softplus(-lam.astype(jnp.float32))
    a = jnp.exp(log_a)
    mult = jnp.sqrt(-jnp.expm1(2.0 * log_a))
    u = mult * i * x.astype(jnp.float32)

    def combine(left, right):
        a_l, h_l = left
        a_r, h_r = right
        return a_l * a_r, a_r * h_l + h_r

    _, h = lax.associative_scan(combine, (a, u), axis=1)
    return h.astype(x.dtype)


def diff_attention(q, k, v, lam, lam_init, subln_g):
    bsz, s = q.shape[0], q.shape[1]
    scale = HEAD_DIM ** -0.5
    chunk_id = jnp.arange(s) // CHUNK
    outs = []
    for q0 in range(0, s, Q_BLOCK):
        k_end = q0 + Q_BLOCK
        qb = q[:, q0:k_end]
        kb = k[:, :k_end]
        vb = v[:, :k_end]
        sc = jnp.einsum('bqhmd,bkhmd->bhmqk', qb, kb).astype(jnp.float32) * scale
        mask = chunk_id[None, :k_end] <= chunk_id[q0:k_end, None]
        sc = jnp.where(mask, sc, MASK_VALUE)
        p = jax.nn.softmax(sc, axis=-1)
        pd = p[:, :, 0] - lam * p[:, :, 1]
        outs.append(jnp.einsum('bhqk,bkhe->bqhe', pd.astype(v.dtype), vb))
    o = jnp.concatenate(outs, axis=1)
    o = rms_norm(o, subln_g, SUBLN_EPS) * (1.0 - lam_init)
    return o.reshape(bsz, s, ATTN_W)


def setup_inputs(seed: int = 0) -> dict:
    key = jax.random.key(seed)
    ks = jax.random.split(key, 24)

    def nrm(k, shape, fan_in):
        return jax.random.normal(k, shape, jnp.float32) * (fan_in ** -0.5)

    def gain(k, shape):
        return 1.0 + 0.01 * jax.random.normal(k, shape, jnp.float32)

    u = jax.random.uniform(ks[9], (DEPTH, D_RNN), jnp.float32, minval=0.9, maxval=0.999)
    root = u ** (1.0 / LRU_C)
    lru_lambda = jnp.log(root) - jnp.log1p(-root)
    return {
        "x": jax.random.normal(ks[0], (BATCH, SEQ, D_MODEL), jnp.float32),
        "norm1_g": gain(ks[1], (DEPTH, D_MODEL)),
        "w_in": nrm(ks[2], (DEPTH, D_MODEL, N_IN), D_MODEL),
        "conv_w": nrm(ks[3], (DEPTH, CONV_W, D_RNN), CONV_W),
        "conv_b": 0.01 * jax.random.normal(ks[4], (DEPTH, D_RNN), jnp.float32),
        "rg_a_w": nrm(ks[5], (DEPTH, RNN_BLOCKS, RNN_BLOCK_W, RNN_BLOCK_W), RNN_BLOCK_W),
        "rg_a_b": 0.01 * jax.random.normal(ks[6], (DEPTH, RNN_BLOCKS, RNN_BLOCK_W), jnp.float32),
        "rg_x_w": nrm(ks[7], (DEPTH, RNN_BLOCKS, RNN_BLOCK_W, RNN_BLOCK_W), RNN_BLOCK_W),
        "rg_x_b": 0.01 * jax.random.normal(ks[8], (DEPTH, RNN_BLOCKS, RNN_BLOCK_W), jnp.float32),
        "lru_lambda": lru_lambda,
        "lambda_q1": 0.1 * jax.random.normal(ks[10], (DEPTH, HEAD_DIM), jnp.float32),
        "lambda_k1": 0.1 * jax.random.normal(ks[11], (DEPTH, HEAD_DIM), jnp.float32),
        "lambda_q2": 0.1 * jax.random.normal(ks[12], (DEPTH, HEAD_DIM), jnp.float32),
        "lambda_k2": 0.1 * jax.random.normal(ks[13], (DEPTH, HEAD_DIM), jnp.float32),
        "subln_g": gain(ks[14], (DEPTH, 2 * HEAD_DIM)),
        "w_br_rnn": nrm(ks[15], (DEPTH, D_RNN, D_MODEL), D_RNN),
        "w_br_attn": nrm(ks[16], (DEPTH, ATTN_W, D_MODEL), ATTN_W),
        "w_out": nrm(ks[17], (DEPTH, D_MODEL, D_MODEL), D_MODEL),
        "norm2_g": gain(ks[18], (DEPTH, D_MODEL)),
        "w_mlp1": nrm(ks[19], (DEPTH, D_MODEL, D_FF), D_MODEL),
        "w_mlp2": nrm(ks[20], (DEPTH, D_FF, D_MODEL), D_FF),
        "normf_g": gain(ks[21], (D_MODEL,)),
    }


def reference(x, norm1_g, w_in, conv_w, conv_b, rg_a_w, rg_a_b, rg_x_w, rg_x_b, lru_lambda,
              lambda_q1, lambda_k1, lambda_q2, lambda_k2, subln_g, w_br_rnn, w_br_attn,
              w_out, norm2_g, w_mlp1, w_mlp2, normf_g):
    bsz, s = x.shape[0], x.shape[1]
    sizes = (D_RNN, D_RNN, ATTN_W, ATTN_W, ATTN_W, D_MODEL, D_MODEL)
    cuts = []
    acc = 0
    for w in sizes[:-1]:
        acc += w
        cuts.append(acc)
    for l in range(DEPTH):
        lam_init = 0.8 - 0.6 * math.exp(-0.3 * l)
        h = rms_norm(x, norm1_g[l])
        proj = h @ w_in[l]
        u_x, u_g, q, k, v, g_r, g_a = jnp.split(proj, cuts, axis=-1)
        xr = causal_depthwise_conv(u_x, conv_w[l], conv_b[l])
        y_rnn = rg_lru(xr, rg_a_w[l], rg_a_b[l], rg_x_w[l], rg_x_b[l], lru_lambda[l]) * jax.nn.gelu(u_g)
        q = rope(q.reshape(bsz, s, 2 * N_HEADS, HEAD_DIM)).reshape(bsz, s, N_HEADS, 2, HEAD_DIM)
        k = rope(k.reshape(bsz, s, 2 * N_HEADS, HEAD_DIM)).reshape(bsz, s, N_HEADS, 2, HEAD_DIM)
        v = v.reshape(bsz, s, N_HEADS, 2 * HEAD_DIM)
        f32 = jnp.float32
        lam = (jnp.exp(jnp.sum(lambda_q1[l].astype(f32) * lambda_k1[l].astype(f32)))
               - jnp.exp(jnp.sum(lambda_q2[l].astype(f32) * lambda_k2[l].astype(f32)))
               + lam_init)
        y_attn = diff_attention(q, k, v, lam, lam_init, subln_g[l])
        merged = (jax.nn.sigmoid(g_r) * (y_rnn @ w_br_rnn[l])
                  + jax.nn.sigmoid(g_a) * (y_attn @ w_br_attn[l]))
        x = x + merged @ w_out[l]
        h2 = rms_norm(x, norm2_g[l])
        x = x + jnp.square(jax.nn.relu(h2 @ w_mlp1[l])) @ w_mlp2[l]
    return rms_norm(x, normf_g)
```

```python
import functools
import math

import jax
import jax.numpy as jnp
from jax import lax
from jax.experimental import pallas as pl
from jax.experimental.pallas import tpu as pltpu

F32 = jnp.float32
BF16 = jnp.bfloat16

CHUNK = 64
CONV_W = 4
LRU_C = 8.0
HEAD_DIM = 64
ROPE_THETA = 10000.0
NORM_EPS = 1e-6
SUBLN_EPS = 1e-5
MASK_VALUE = -1e30

LANES = 128
SUBLANES = 8
MXU_DIM = 256
VMEM_LIMIT = 56 * 1024 * 1024

PROJ_ROWS = 512
SCAN_ROWS = 256
ATT_Q = 256
ATT_K = 256
MLP_ROWS = 256


def _rms(x, g, eps):
    ms = jnp.mean(x * x, axis=-1, keepdims=True)
    return x * lax.rsqrt(ms + eps) * g


def _whole(memory_space=pltpu.VMEM):
    return pl.BlockSpec(memory_space=memory_space)


def _rope_cols(y, cos, sin_signed, first_half):
    outs = []
    for c in range(y.shape[1] // LANES):
        blk = y[:, c * LANES:(c + 1) * LANES]
        rot = jnp.where(first_half,
                        pltpu.roll(blk, LANES - HEAD_DIM // 2, axis=1),
                        pltpu.roll(blk, HEAD_DIM // 2, axis=1))
        outs.append(blk * cos + rot * sin_signed)
    return jnp.concatenate(outs, axis=1)


def _inproj_kernel(x_ref, g_ref, w_ref, cos_ref, sin_ref,
                   ux_ref, gg_ref, q_ref, k_ref, v_ref, sr_ref, sa_ref, *, d):
    h = _rms(x_ref[...], g_ref[...], NORM_EPS).astype(BF16)

    def proj(seg):
        return jnp.dot(h, w_ref[:, seg * d:(seg + 1) * d], preferred_element_type=F32)

    cos = cos_ref[...]
    sin_signed = sin_ref[...]
    lane = lax.broadcasted_iota(jnp.int32, cos.shape, 1)
    first_half = (lane % HEAD_DIM) < (HEAD_DIM // 2)

    ux_ref[...] = proj(0)
    gg_ref[...] = jax.nn.gelu(proj(1))
    q_ref[...] = (_rope_cols(proj(2), cos, sin_signed, first_half)
                  * (HEAD_DIM ** -0.5)).astype(BF16)
    k_ref[...] = _rope_cols(proj(3), cos, sin_signed, first_half).astype(BF16)
    v_ref[...] = proj(4).astype(BF16)
    sr_ref[...] = jax.nn.sigmoid(proj(5))
    sa_ref[...] = jax.nn.sigmoid(proj(6))


def _inproj(x2, g, w_bf, cos_t, sin_t, seq):
    t, d = x2.shape
    tm = PROJ_ROWS
    per_seq = seq // tm
    row = pl.BlockSpec((tm, d), lambda i: (i, 0))
    tab = pl.BlockSpec((tm, LANES), lambda i: (i % per_seq, 0))
    f32o = jax.ShapeDtypeStruct((t, d), F32)
    bf16o = jax.ShapeDtypeStruct((t, d), BF16)
    return pl.pallas_call(
        functools.partial(_inproj_kernel, d=d),
        grid=(t // tm,),
        in_specs=[row, _whole(), _whole(), tab, tab],
        out_specs=[row] * 7,
        out_shape=[f32o, f32o, bf16o, bf16o, bf16o, f32o, f32o],
        compiler_params=pltpu.CompilerParams(
            dimension_semantics=("arbitrary",), vmem_limit_bytes=VMEM_LIMIT),
        name="inproj",
    )(x2, g, w_bf, cos_t, sin_t)


def _rglru_kernel(ux_ref, gg_ref, cw_ref, cb_ref, wa_ref, ba_ref, wx_ref, bx_ref,
                  lam_ref, y_ref, ext_ref, hc_ref, *, ts, d):
    @pl.when(pl.program_id(1) == 0)
    def _():
        ext_ref[0:SUBLANES, :] = jnp.zeros((SUBLANES, d), F32)
        hc_ref[...] = jnp.zeros((SUBLANES, d), F32)

    ext_ref[SUBLANES:, :] = ux_ref[...]
    cw = cw_ref[...]
    xr = ext_ref[SUBLANES - 3:SUBLANES - 3 + ts, :] * cw[0:1, :]
    for j in range(1, CONV_W):
        o = SUBLANES - (CONV_W - 1) + j
        xr = xr + ext_ref[o:o + ts, :] * cw[j:j + 1, :]
    xr = xr + cb_ref[...]
    ext_ref[0:SUBLANES, :] = ext_ref[ts:ts + SUBLANES, :]

    xb = xr.astype(BF16)
    ra, ri = [], []
    for g in range(d // MXU_DIM):
        xg = xb[:, g * MXU_DIM:(g + 1) * MXU_DIM]
        ra.append(jnp.dot(xg, wa_ref[g], preferred_element_type=F32))
        ri.append(jnp.dot(xg, wx_ref[g], preferred_element_type=F32))
    r = jax.nn.sigmoid(jnp.concatenate(ra, axis=1) + ba_ref[...])
    i = jax.nn.sigmoid(jnp.concatenate(ri, axis=1) + bx_ref[...])

    log_a = (-LRU_C * jax.nn.softplus(-lam_ref[...])) * r
    a = jnp.exp(log_a)
    u = jnp.sqrt(-jnp.tanh(log_a) * (a * a + 1.0)) * i * xr

    groups = ts // SUBLANES
    a3 = a.reshape(groups, SUBLANES, d)
    h3 = u.reshape(groups, SUBLANES, d)
    sub = lax.broadcasted_iota(jnp.int32, (groups, SUBLANES, d), 1)
    step = 1
    while step < SUBLANES:
        keep = sub >= step
        a_prev = jnp.where(keep, pltpu.roll(a3, step, axis=1), 1.0)
        h_prev = jnp.where(keep, pltpu.roll(h3, step, axis=1), 0.0)
        h3 = h3 + a3 * h_prev
        a3 = a3 * a_prev
        step *= 2

    carry = hc_ref[...]
    outs = []
    for g in range(groups):
        hg = h3[g] + a3[g] * carry
        outs.append(hg)
        carry = jnp.broadcast_to(hg[SUBLANES - 1:SUBLANES, :], (SUBLANES, d))
    hc_ref[...] = carry
    h = jnp.concatenate(outs, axis=0)
    y_ref[...] = (h * gg_ref[...]).astype(BF16)


def _rglru(ux, gg, cw, cb, wa_bd, ba, wx_bd, bx, lam, batch, seq):
    t, d = ux.shape
    ts = SCAN_ROWS
    per_seq = seq // ts
    row = pl.BlockSpec((ts, d), lambda b, s: (b * per_seq + s, 0))
    return pl.pallas_call(
        functools.partial(_rglru_kernel, ts=ts, d=d),
        grid=(batch, per_seq),
        in_specs=[row, row] + [_whole()] * 7,
        out_specs=row,
        out_shape=jax.ShapeDtypeStruct((t, d), BF16),
        scratch_shapes=[pltpu.VMEM((ts + SUBLANES, d), F32),
                        pltpu.VMEM((SUBLANES, d), F32)],
        compiler_params=pltpu.CompilerParams(
            dimension_semantics=("arbitrary", "arbitrary"), vmem_limit_bytes=VMEM_LIMIT),
        name="rglru",
    )(ux, gg, cw, cb, wa_bd, ba, wx_bd, bx, lam)


def _attn_kernel(lam_ref, g_ref, q_ref, k_ref, v_ref, o_ref, m_ref, l_ref, acc_ref,
                 *, seq, lam_init):
    tq, tk = ATT_Q, ATT_K
    lv = lam_ref[...]
    lam = (jnp.exp(jnp.sum(lv[0:1, :] * lv[1:2, :], axis=-1, keepdims=True))
           - jnp.exp(jnp.sum(lv[2:3, :] * lv[3:4, :], axis=-1, keepdims=True))
           + lam_init)
    lane = lax.broadcasted_iota(jnp.int32, (tq, LANES), 1)
    first_sub = lane < HEAD_DIM
    row_c = lax.broadcasted_iota(jnp.int32, (2 * tq, tk), 0) % tq // CHUNK
    col_c = lax.broadcasted_iota(jnp.int32, (2 * tq, tk), 1) // CHUNK
    diag_ok = col_c <= row_c

    def tile(qq, j, masked):
        k_t = k_ref[pl.ds(j * tk, tk), :]
        v_t = v_ref[pl.ds(j * tk, tk), :]
        s = lax.dot_general(qq, k_t, (((1,), (1,)), ((), ())),
                            preferred_element_type=F32)
        if masked:
            s = jnp.where(diag_ok, s, MASK_VALUE)
        m_old = m_ref[...]
        m_new = jnp.maximum(m_old, jnp.max(s, axis=-1, keepdims=True))
        alpha = jnp.exp(m_old - m_new)
        p = jnp.exp(s - m_new)
        l_ref[...] = alpha * l_ref[...] + jnp.sum(p, axis=-1, keepdims=True)
        acc_ref[...] = alpha * acc_ref[...] + jnp.dot(
            p.astype(BF16), v_t, preferred_element_type=F32)
        m_ref[...] = m_new

    for i in range(seq // tq):
        q = q_ref[i * tq:(i + 1) * tq, :]
        zero = jnp.zeros_like(q)
        qq = jnp.concatenate([jnp.where(first_sub, q, zero),
                              jnp.where(first_sub, zero, q)], axis=0)
        m_ref[...] = jnp.full(m_ref.shape, MASK_VALUE, F32)
        l_ref[...] = jnp.zeros(l_ref.shape, F32)
        acc_ref[...] = jnp.zeros(acc_ref.shape, F32)
        if i > 0:
            def body(j, c, qq=qq):
                tile(qq, j, False)
                return c
            lax.fori_loop(0, i, body, 0)
        tile(qq, i, True)
        inv_l = 1.0 / l_ref[...]
        o = acc_ref[...] * inv_l
        o = o[0:tq, :] - lam * o[tq:2 * tq, :]
        o = _rms(o, g_ref[...], SUBLN_EPS) * (1.0 - lam_init)
        o_ref[i * tq:(i + 1) * tq, :] = o.astype(BF16)


def _attention(lam_vecs, subln_g, q, k, v, lam_init):
    b, s, w = q.shape
    heads = w // LANES
    blk = pl.BlockSpec((None, s, LANES), lambda bi, hi: (bi, 0, hi))
    return pl.pallas_call(
        functools.partial(_attn_kernel, seq=s, lam_init=lam_init),
        grid=(b, heads),
        in_specs=[_whole(), _whole(), blk, blk, blk],
        out_specs=blk,
        out_shape=jax.ShapeDtypeStruct((b, s, w), BF16),
        scratch_shapes=[pltpu.VMEM((2 * ATT_Q, 1), F32),
                        pltpu.VMEM((2 * ATT_Q, 1), F32),
                        pltpu.VMEM((2 * ATT_Q, LANES), F32)],
        compiler_params=pltpu.CompilerParams(
            dimension_semantics=("arbitrary", "arbitrary"), vmem_limit_bytes=VMEM_LIMIT),
        name="diffattn",
    )(lam_vecs, subln_g, q, k, v)


def _mlp_kernel(x_ref, yr_ref, ya_ref, sr_ref, sa_ref, wr_ref, wa_ref, wo_ref,
                g2_ref, w1_ref, w2_ref, gf_ref, o_ref, *, d, d_ff, final_norm):
    br = jnp.dot(yr_ref[...], wr_ref[...], preferred_element_type=F32)
    ba = jnp.dot(ya_ref[...], wa_ref[...], preferred_element_type=F32)
    merged = (sr_ref[...] * br + sa_ref[...] * ba).astype(BF16)
    x1 = x_ref[...] + jnp.dot(merged, wo_ref[...], preferred_element_type=F32)
    h2 = _rms(x1, g2_ref[...], NORM_EPS).astype(BF16)
    x2 = x1
    for c in range(d_ff // d):
        hid = jnp.dot(h2, w1_ref[:, c * d:(c + 1) * d], preferred_element_type=F32)
        hid = jnp.square(jnp.maximum(hid, 0.0)).astype(BF16)
        x2 = x2 + jnp.dot(hid, w2_ref[c * d:(c + 1) * d, :], preferred_element_type=F32)
    if final_norm:
        x2 = _rms(x2, gf_ref[...], NORM_EPS)
    o_ref[...] = x2


def _merge_mlp(x2, yr, ya, sr, sa, wr, wa, wo, g2, w1, w2, gf, final_norm):
    t, d = x2.shape
    d_ff = w1.shape[1]
    tm = MLP_ROWS
    row = pl.BlockSpec((tm, d), lambda i: (i, 0))
    return pl.pallas_call(
        functools.partial(_mlp_kernel, d=d, d_ff=d_ff, final_norm=final_norm),
        grid=(t // tm,),
        in_specs=[row] * 5 + [_whole()] * 7,
        out_specs=row,
        out_shape=jax.ShapeDtypeStruct((t, d), F32),
        compiler_params=pltpu.CompilerParams(
            dimension_semantics=("arbitrary",), vmem_limit_bytes=VMEM_LIMIT),
        name="merge_mlp",
    )(x2, yr, ya, sr, sa, wr, wa, wo, g2, w1, w2, gf)


def _rope_tables(seq):
    half = HEAD_DIM // 2
    inv_freq = ROPE_THETA ** (-jnp.arange(half, dtype=F32) * 2.0 / HEAD_DIM)
    ang = jnp.arange(seq, dtype=F32)[:, None] * inv_freq[None, :]
    cos, sin = jnp.cos(ang), jnp.sin(ang)
    reps = LANES // HEAD_DIM
    cos_t = jnp.tile(jnp.concatenate([cos, cos], axis=1), (1, reps))
    sin_t = jnp.tile(jnp.concatenate([-sin, sin], axis=1), (1, reps))
    return cos_t, sin_t


def _block_diag(w):
    n, bw, _ = w.shape
    per = MXU_DIM // bw
    w = w.reshape(n // per, per, bw, bw)
    eye = jnp.eye(per, dtype=w.dtype)
    return jnp.einsum('gpij,pq->gpiqj', w, eye).reshape(n // per, MXU_DIM, MXU_DIM)


def kernel(x, norm1_g, w_in, conv_w, conv_b, rg_a_w, rg_a_b, rg_x_w, rg_x_b, lru_lambda,
           lambda_q1, lambda_k1, lambda_q2, lambda_k2, subln_g, w_br_rnn, w_br_attn,
           w_out, norm2_g, w_mlp1, w_mlp2, normf_g):
    bsz, seq, d = x.shape
    depth = w_in.shape[0]
    t = bsz * seq
    cos_t, sin_t = _rope_tables(seq)
    xf = x.reshape(t, d)
    for l in range(depth):
        lam_init = 0.8 - 0.6 * math.exp(-0.3 * l)
        ux, gg, q, k, v, sr, sa = _inproj(
            xf, norm1_g[l].reshape(1, d), w_in[l].astype(BF16), cos_t, sin_t, seq)
        y_rnn = _rglru(
            ux, gg, conv_w[l], conv_b[l].reshape(1, d),
            _block_diag(rg_a_w[l]).astype(BF16), rg_a_b[l].reshape(1, d),
            _block_diag(rg_x_w[l]).astype(BF16), rg_x_b[l].reshape(1, d),
            lru_lambda[l].reshape(1, d), bsz, seq)
        lam_vecs = jnp.stack([lambda_q1[l], lambda_k1[l], lambda_q2[l], lambda_k2[l]])
        y_attn = _attention(
            lam_vecs.astype(F32), subln_g[l].reshape(1, LANES),
            q.reshape(bsz, seq, d), k.reshape(bsz, seq, d), v.reshape(bsz, seq, d),
            lam_init)
        xf = _merge_mlp(
            xf, y_rnn, y_attn.reshape(t, d), sr, sa,
            w_br_rnn[l].astype(BF16), w_br_attn[l].astype(BF16), w_out[l].astype(BF16),
            norm2_g[l].reshape(1, d), w_mlp1[l].astype(BF16), w_mlp2[l].astype(BF16),
            normf_g.reshape(1, d), final_norm=(l == depth - 1))
    return xf.reshape(bsz, seq, d)
```

```python
import functools
import math

import jax
import jax.numpy as jnp
from jax import lax
from jax.experimental import pallas as pl
from jax.experimental.pallas import tpu as pltpu

F32 = jnp.float32
BF16 = jnp.bfloat16

CHUNK = 64
CONV_W = 4
LRU_C = 8.0
HEAD_DIM = 64
ROPE_THETA = 10000.0
NORM_EPS = 1e-6
SUBLN_EPS = 1e-5
MASK_VALUE = -1e30

LANES = 128
SUBLANES = 8
MXU_DIM = 256
VMEM_LIMIT = 56 * 1024 * 1024

PROJ_ROWS = 512
SCAN_ROWS = 256
ATT_Q = 256
MLP_ROWS = 256


def _rms(x, g, eps):
    ms = jnp.mean(x * x, axis=-1, keepdims=True)
    return x * lax.rsqrt(ms + eps) * g


def _whole(memory_space=pltpu.VMEM):
    return pl.BlockSpec(memory_space=memory_space)


def _rope_cols(y, cos, sin_signed, first_half):
    outs = []
    for c in range(y.shape[1] // LANES):
        blk = y[:, c * LANES:(c + 1) * LANES]
        rot = jnp.where(first_half,
                        pltpu.roll(blk, LANES - HEAD_DIM // 2, axis=1),
                        pltpu.roll(blk, HEAD_DIM // 2, axis=1))
        outs.append(blk * cos + rot * sin_signed)
    return jnp.concatenate(outs, axis=1)


def _store_head_tiles(ref, y):
    yt = y.T
    for hd in range(ref.shape[0]):
        for j in range(ref.shape[1]):
            ref[hd, j] = yt[hd * LANES:(hd + 1) * LANES,
                            j * ATT_Q:(j + 1) * ATT_Q].astype(BF16)


def _inproj_kernel(x_ref, g_ref, w_ref, cos_ref, sin_ref,
                   ux_ref, gg_ref, qt_ref, k_ref, vt_ref, sr_ref, sa_ref, *, d):
    h = _rms(x_ref[...], g_ref[...], NORM_EPS).astype(BF16)

    def proj(seg):
        return jnp.dot(h, w_ref[:, seg * d:(seg + 1) * d], preferred_element_type=F32)

    cos = cos_ref[...]
    sin_signed = sin_ref[...]
    lane = lax.broadcasted_iota(jnp.int32, cos.shape, 1)
    first_half = (lane % HEAD_DIM) < (HEAD_DIM // 2)

    ux_ref[...] = proj(0)
    gg_ref[...] = jax.nn.gelu(proj(1))
    _store_head_tiles(qt_ref, _rope_cols(proj(2), cos, sin_signed, first_half)
                      * (HEAD_DIM ** -0.5))
    k_ref[...] = _rope_cols(proj(3), cos, sin_signed, first_half).astype(BF16)
    _store_head_tiles(vt_ref, proj(4))
    sr_ref[...] = jax.nn.sigmoid(proj(5))
    sa_ref[...] = jax.nn.sigmoid(proj(6))


def _inproj(x2, g, w_bf, cos_t, sin_t, batch, seq):
    t, d = x2.shape
    tm = PROJ_ROWS
    per_seq = seq // tm
    heads = d // LANES
    row = pl.BlockSpec((tm, d), lambda i: (i, 0))
    tab = pl.BlockSpec((tm, LANES), lambda i: (i % per_seq, 0))
    tiles = pl.BlockSpec((None, heads, tm // ATT_Q, LANES, ATT_Q),
                         lambda i: (i // per_seq, 0, i % per_seq, 0, 0))
    f32o = jax.ShapeDtypeStruct((t, d), F32)
    bf16o = jax.ShapeDtypeStruct((t, d), BF16)
    tileo = jax.ShapeDtypeStruct((batch, heads, seq // ATT_Q, LANES, ATT_Q), BF16)
    return pl.pallas_call(
        functools.partial(_inproj_kernel, d=d),
        grid=(t // tm,),
        in_specs=[row, _whole(), _whole(), tab, tab],
        out_specs=[row, row, tiles, row, tiles, row, row],
        out_shape=[f32o, f32o, tileo, bf16o, tileo, f32o, f32o],
        compiler_params=pltpu.CompilerParams(
            dimension_semantics=("arbitrary",), vmem_limit_bytes=VMEM_LIMIT),
        name="inproj",
    )(x2, g, w_bf, cos_t, sin_t)


def _rglru_kernel(ux_ref, gg_ref, cw_ref, cb_ref, wa_ref, ba_ref, wx_ref, bx_ref,
                  lam_ref, y_ref, ext_ref, hc_ref, *, ts, d):
    @pl.when(pl.program_id(1) == 0)
    def _():
        ext_ref[0:SUBLANES, :] = jnp.zeros((SUBLANES, d), F32)
        hc_ref[...] = jnp.zeros((SUBLANES, d), F32)

    ext_ref[SUBLANES:, :] = ux_ref[...]
    cw = cw_ref[...]
    xr = ext_ref[SUBLANES - 3:SUBLANES - 3 + ts, :] * cw[0:1, :]
    for j in range(1, CONV_W):
        o = SUBLANES - (CONV_W - 1) + j
        xr = xr + ext_ref[o:o + ts, :] * cw[j:j + 1, :]
    xr = xr + cb_ref[...]
    ext_ref[0:SUBLANES, :] = ext_ref[ts:ts + SUBLANES, :]

    xb = xr.astype(BF16)
    ra, ri = [], []
    for g in range(d // MXU_DIM):
        xg = xb[:, g * MXU_DIM:(g + 1) * MXU_DIM]
        ra.append(jnp.dot(xg, wa_ref[g], preferred_element_type=F32))
        ri.append(jnp.dot(xg, wx_ref[g], preferred_element_type=F32))
    r = jax.nn.sigmoid(jnp.concatenate(ra, axis=1) + ba_ref[...])
    i = jax.nn.sigmoid(jnp.concatenate(ri, axis=1) + bx_ref[...])

    log_a = (-LRU_C * jax.nn.softplus(-lam_ref[...])) * r
    a = jnp.exp(log_a)
    u = jnp.sqrt(-jnp.tanh(log_a) * (a * a + 1.0)) * i * xr

    groups = ts // SUBLANES
    a3 = a.reshape(groups, SUBLANES, d)
    h3 = u.reshape(groups, SUBLANES, d)
    sub = lax.broadcasted_iota(jnp.int32, (groups, SUBLANES, d), 1)
    step = 1
    while step < SUBLANES:
        keep = sub >= step
        a_prev = jnp.where(keep, pltpu.roll(a3, step, axis=1), 1.0)
        h_prev = jnp.where(keep, pltpu.roll(h3, step, axis=1), 0.0)
        h3 = h3 + a3 * h_prev
        a3 = a3 * a_prev
        step *= 2

    carry = hc_ref[...]
    outs = []
    for g in range(groups):
        hg = h3[g] + a3[g] * carry
        outs.append(hg)
        carry = jnp.broadcast_to(hg[SUBLANES - 1:SUBLANES, :], (SUBLANES, d))
    hc_ref[...] = carry
    h = jnp.concatenate(outs, axis=0)
    y_ref[...] = (h * gg_ref[...]).astype(BF16)


def _rglru(ux, gg, cw, cb, wa_bd, ba, wx_bd, bx, lam, batch, seq):
    t, d = ux.shape
    ts = SCAN_ROWS
    per_seq = seq // ts
    row = pl.BlockSpec((ts, d), lambda b, s: (b * per_seq + s, 0))
    return pl.pallas_call(
        functools.partial(_rglru_kernel, ts=ts, d=d),
        grid=(batch, per_seq),
        in_specs=[row, row] + [_whole()] * 7,
        out_specs=row,
        out_shape=jax.ShapeDtypeStruct((t, d), BF16),
        scratch_shapes=[pltpu.VMEM((ts + SUBLANES, d), F32),
                        pltpu.VMEM((SUBLANES, d), F32)],
        compiler_params=pltpu.CompilerParams(
            dimension_semantics=("arbitrary", "arbitrary"), vmem_limit_bytes=VMEM_LIMIT),
        name="rglru",
    )(ux, gg, cw, cb, wa_bd, ba, wx_bd, bx, lam)


def _attn_kernel(lam_ref, g_ref, qt_ref, k_ref, vt_ref, o_ref, s_ref, *, lam_init):
    n_tiles, feat, tq = qt_ref.shape
    lv = lam_ref[...]
    lam = (jnp.exp(jnp.sum(lv[0:1, :] * lv[1:2, :], axis=-1, keepdims=True))
           - jnp.exp(jnp.sum(lv[2:3, :] * lv[3:4, :], axis=-1, keepdims=True))
           + lam_init)
    first_sub = lax.broadcasted_iota(jnp.int32, (feat, tq), 0) < HEAD_DIM
    key_c = lax.broadcasted_iota(jnp.int32, (tq, 2 * tq), 0) // CHUNK
    qry_c = lax.broadcasted_iota(jnp.int32, (tq, 2 * tq), 1) % tq // CHUNK
    diag_ok = key_c <= qry_c

    for i in range(n_tiles):
        qt = qt_ref[i].astype(F32)
        qq = jnp.concatenate([jnp.where(first_sub, qt, 0.0),
                              jnp.where(first_sub, 0.0, qt)], axis=1).astype(BF16)
        mx = None
        for c in range(i + 1):
            s = jnp.dot(k_ref[c * tq:(c + 1) * tq, :], qq,
                        preferred_element_type=F32)
            if c == i:
                s = jnp.where(diag_ok, s, MASK_VALUE)
            s_ref[c] = s
            part = jnp.max(s.reshape(tq // SUBLANES, SUBLANES, 2 * tq), axis=0)
            mx = part if mx is None else jnp.maximum(mx, part)
        m = jnp.max(mx, axis=0, keepdims=True)
        lsum = jnp.zeros((SUBLANES, 2 * tq), F32)
        acc = jnp.zeros((feat, 2 * tq), F32)
        for c in range(i + 1):
            p = jnp.exp(s_ref[c] - m)
            lsum = lsum + jnp.sum(p.reshape(tq // SUBLANES, SUBLANES, 2 * tq), axis=0)
            acc = acc + jnp.dot(vt_ref[c], p.astype(BF16), preferred_element_type=F32)
        inv_l = 1.0 / jnp.sum(lsum, axis=0, keepdims=True)
        acc = acc * inv_l
        o = (acc[:, 0:tq] - lam * acc[:, tq:2 * tq]).T
        o = _rms(o, g_ref[...], SUBLN_EPS) * (1.0 - lam_init)
        o_ref[i * tq:(i + 1) * tq, :] = o.astype(BF16)


def _attention(lam_vecs, subln_g, qt, k, vt, lam_init):
    b, s, w = k.shape
    heads, n_tiles = qt.shape[1], qt.shape[2]
    tiles = pl.BlockSpec((None, None, n_tiles, LANES, ATT_Q), lambda bi, hi: (bi, hi, 0, 0, 0))
    rows = pl.BlockSpec((None, s, LANES), lambda bi, hi: (bi, 0, hi))
    return pl.pallas_call(
        functools.partial(_attn_kernel, lam_init=lam_init),
        grid=(b, heads),
        in_specs=[_whole(), _whole(), tiles, rows, tiles],
        out_specs=rows,
        out_shape=jax.ShapeDtypeStruct((b, s, w), BF16),
        scratch_shapes=[pltpu.VMEM((n_tiles, ATT_Q, 2 * ATT_Q), F32)],
        compiler_params=pltpu.CompilerParams(
            dimension_semantics=("arbitrary", "arbitrary"), vmem_limit_bytes=VMEM_LIMIT),
        name="diffattn",
    )(lam_vecs, subln_g, qt, k, vt)


def _mlp_kernel(x_ref, yr_ref, ya_ref, sr_ref, sa_ref, wr_ref, wa_ref, wo_ref,
                g2_ref, w1_ref, w2_ref, gf_ref, o_ref, *, d, d_ff, final_norm):
    br = jnp.dot(yr_ref[...], wr_ref[...], preferred_element_type=F32)
    ba = jnp.dot(ya_ref[...], wa_ref[...], preferred_element_type=F32)
    merged = (sr_ref[...] * br + sa_ref[...] * ba).astype(BF16)
    x1 = x_ref[...] + jnp.dot(merged, wo_ref[...], preferred_element_type=F32)
    h2 = _rms(x1, g2_ref[...], NORM_EPS).astype(BF16)
    x2 = x1
    for c in range(d_ff // d):
        hid = jnp.dot(h2, w1_ref[:, c * d:(c + 1) * d], preferred_element_type=F32)
        hid = jnp.square(jnp.maximum(hid, 0.0)).astype(BF16)
        x2 = x2 + jnp.dot(hid, w2_ref[c * d:(c + 1) * d, :], preferred_element_type=F32)
    if final_norm:
        x2 = _rms(x2, gf_ref[...], NORM_EPS)
    o_ref[...] = x2


def _merge_mlp(x2, yr, ya, sr, sa, wr, wa, wo, g2, w1, w2, gf, final_norm):
    t, d = x2.shape
    d_ff = w1.shape[1]
    tm = MLP_ROWS
    row = pl.BlockSpec((tm, d), lambda i: (i, 0))
    return pl.pallas_call(
        functools.partial(_mlp_kernel, d=d, d_ff=d_ff, final_norm=final_norm),
        grid=(t // tm,),
        in_specs=[row] * 5 + [_whole()] * 7,
        out_specs=row,
        out_shape=jax.ShapeDtypeStruct((t, d), F32),
        compiler_params=pltpu.CompilerParams(
            dimension_semantics=("arbitrary",), vmem_limit_bytes=VMEM_LIMIT),
        name="merge_mlp",
    )(x2, yr, ya, sr, sa, wr, wa, wo, g2, w1, w2, gf)


def _rope_tables(seq):
    half = HEAD_DIM // 2
    inv_freq = ROPE_THETA ** (-jnp.arange(half, dtype=F32) * 2.0 / HEAD_DIM)
    ang = jnp.arange(seq, dtype=F32)[:, None] * inv_freq[None, :]
    cos, sin = jnp.cos(ang), jnp.sin(ang)
    reps = LANES // HEAD_DIM
    cos_t = jnp.tile(jnp.concatenate([cos, cos], axis=1), (1, reps))
    sin_t = jnp.tile(jnp.concatenate([-sin, sin], axis=1), (1, reps))
    return cos_t, sin_t


def _block_diag(w):
    n, bw, _ = w.shape
    per = MXU_DIM // bw
    w = w.reshape(n // per, per, bw, bw)
    eye = jnp.eye(per, dtype=w.dtype)
    return jnp.einsum('gpij,pq->gpiqj', w, eye).reshape(n // per, MXU_DIM, MXU_DIM)


def kernel(x, norm1_g, w_in, conv_w, conv_b, rg_a_w, rg_a_b, rg_x_w, rg_x_b, lru_lambda,
           lambda_q1, lambda_k1, lambda_q2, lambda_k2, subln_g, w_br_rnn, w_br_attn,
           w_out, norm2_g, w_mlp1, w_mlp2, normf_g):
    bsz, seq, d = x.shape
    depth = w_in.shape[0]
    t = bsz * seq
    cos_t, sin_t = _rope_tables(seq)
    xf = x.reshape(t, d)
    for l in range(depth):
        lam_init = 0.8 - 0.6 * math.exp(-0.3 * l)
        ux, gg, qt, k, vt, sr, sa = _inproj(
            xf, norm1_g[l].reshape(1, d), w_in[l].astype(BF16), cos_t, sin_t, bsz, seq)
        y_rnn = _rglru(
            ux, gg, conv_w[l], conv_b[l].reshape(1, d),
            _block_diag(rg_a_w[l]).astype(BF16), rg_a_b[l].reshape(1, d),
            _block_diag(rg_x_w[l]).astype(BF16), rg_x_b[l].reshape(1, d),
            lru_lambda[l].reshape(1, d), bsz, seq)
        lam_vecs = jnp.stack([lambda_q1[l], lambda_k1[l], lambda_q2[l], lambda_k2[l]])
        y_attn = _attention(
            lam_vecs.astype(F32), subln_g[l].reshape(1, LANES),
            qt, k.reshape(bsz, seq, d), vt, lam_init)
        xf = _merge_mlp(
            xf, y_rnn, y_attn.reshape(t, d), sr, sa,
            w_br_rnn[l].astype(BF16), w_br_attn[l].astype(BF16), w_out[l].astype(BF16),
            norm2_g[l].reshape(1, d), w_mlp1[l].astype(BF16), w_mlp2[l].astype(BF16),
            normf_g.reshape(1, d), final_norm=(l == depth - 1))
    return xf.reshape(bsz, seq, d)
```

```python
import functools
import math

import jax
import jax.numpy as jnp
from jax import lax
from jax.experimental import pallas as pl
from jax.experimental.pallas import tpu as pltpu

F32 = jnp.float32
BF16 = jnp.bfloat16

CHUNK = 64
CONV_W = 4
LRU_C = 8.0
HEAD_DIM = 64
ROPE_THETA = 10000.0
NORM_EPS = 1e-6
SUBLN_EPS = 1e-5
MASK_VALUE = -1e30
LOG2_E = 1.4426950408889634

LANES = 128
SUBLANES = 8
MXU_DIM = 256
VMEM_LIMIT = 56 * 1024 * 1024

PROJ_ROWS = 512
SCAN_ROWS = 256
ATT_Q = 256
ATT_HEADS = 1
MLP_ROWS = 256


def _rms(x, g, eps):
    ms = jnp.mean(x * x, axis=-1, keepdims=True)
    return x * lax.rsqrt(ms + eps) * g


def _whole(memory_space=pltpu.VMEM):
    return pl.BlockSpec(memory_space=memory_space)


def _rope_cols(y, cos, sin_signed, first_half):
    outs = []
    for c in range(y.shape[1] // LANES):
        blk = y[:, c * LANES:(c + 1) * LANES]
        rot = jnp.where(first_half,
                        pltpu.roll(blk, LANES - HEAD_DIM // 2, axis=1),
                        pltpu.roll(blk, HEAD_DIM // 2, axis=1))
        outs.append(blk * cos + rot * sin_signed)
    return jnp.concatenate(outs, axis=1)


def _store_head_tiles(ref, y):
    yt = y.T
    for hd in range(ref.shape[0]):
        for j in range(ref.shape[1]):
            ref[hd, j] = yt[hd * LANES:(hd + 1) * LANES,
                            j * ATT_Q:(j + 1) * ATT_Q].astype(BF16)


def _store_head_rows(ref, y):
    yt = y.T
    for hd in range(ref.shape[0]):
        ref[hd] = yt[hd * LANES:(hd + 1) * LANES, :].astype(BF16)


def _inproj_kernel(x_ref, g_ref, w_ref, cos_ref, sin_ref,
                   ux_ref, gg_ref, qt_ref, k_ref, vt_ref, sr_ref, sa_ref, *, d):
    h = _rms(x_ref[...], g_ref[...], NORM_EPS).astype(BF16)

    def proj(seg):
        return jnp.dot(h, w_ref[:, seg * d:(seg + 1) * d], preferred_element_type=F32)

    cos = cos_ref[...]
    sin_signed = sin_ref[...]
    lane = lax.broadcasted_iota(jnp.int32, cos.shape, 1)
    first_half = (lane % HEAD_DIM) < (HEAD_DIM // 2)

    ux_ref[...] = proj(0)
    gg_ref[...] = jax.nn.gelu(proj(1))
    _store_head_tiles(qt_ref, _rope_cols(proj(2), cos, sin_signed, first_half)
                      * (HEAD_DIM ** -0.5 * LOG2_E))
    k_ref[...] = _rope_cols(proj(3), cos, sin_signed, first_half).astype(BF16)
    _store_head_rows(vt_ref, proj(4))
    sr_ref[...] = jax.nn.sigmoid(proj(5))
    sa_ref[...] = jax.nn.sigmoid(proj(6))


def _inproj(x2, g, w_bf, cos_t, sin_t, batch, seq):
    t, d = x2.shape
    tm = PROJ_ROWS
    per_seq = seq // tm
    heads = d // LANES
    row = pl.BlockSpec((tm, d), lambda i: (i, 0))
    tab = pl.BlockSpec((tm, LANES), lambda i: (i % per_seq, 0))
    tiles = pl.BlockSpec((None, heads, tm // ATT_Q, LANES, ATT_Q),
                         lambda i: (i // per_seq, 0, i % per_seq, 0, 0))
    cols = pl.BlockSpec((None, heads, LANES, tm), lambda i: (i // per_seq, 0, 0, i % per_seq))
    f32o = jax.ShapeDtypeStruct((t, d), F32)
    bf16o = jax.ShapeDtypeStruct((t, d), BF16)
    tileo = jax.ShapeDtypeStruct((batch, heads, seq // ATT_Q, LANES, ATT_Q), BF16)
    colo = jax.ShapeDtypeStruct((batch, heads, LANES, seq), BF16)
    return pl.pallas_call(
        functools.partial(_inproj_kernel, d=d),
        grid=(t // tm,),
        in_specs=[row, _whole(), _whole(), tab, tab],
        out_specs=[row, row, tiles, row, cols, row, row],
        out_shape=[f32o, f32o, tileo, bf16o, colo, f32o, f32o],
        compiler_params=pltpu.CompilerParams(
            dimension_semantics=("arbitrary",), vmem_limit_bytes=VMEM_LIMIT),
        name="inproj",
    )(x2, g, w_bf, cos_t, sin_t)


def _rglru_kernel(ux_ref, gg_ref, cw_ref, cb_ref, wa_ref, ba_ref, wx_ref, bx_ref,
                  lam_ref, y_ref, ext_ref, hc_ref, *, ts, d):
    @pl.when(pl.program_id(1) == 0)
    def _():
        ext_ref[0:SUBLANES, :] = jnp.zeros((SUBLANES, d), F32)
        hc_ref[...] = jnp.zeros((SUBLANES, d), F32)

    ext_ref[SUBLANES:, :] = ux_ref[...]
    cw = cw_ref[...]
    xr = ext_ref[SUBLANES - 3:SUBLANES - 3 + ts, :] * cw[0:1, :]
    for j in range(1, CONV_W):
        o = SUBLANES - (CONV_W - 1) + j
        xr = xr + ext_ref[o:o + ts, :] * cw[j:j + 1, :]
    xr = xr + cb_ref[...]
    ext_ref[0:SUBLANES, :] = ext_ref[ts:ts + SUBLANES, :]

    xb = xr.astype(BF16)
    ra, ri = [], []
    for g in range(d // MXU_DIM):
        xg = xb[:, g * MXU_DIM:(g + 1) * MXU_DIM]
        ra.append(jnp.dot(xg, wa_ref[g], preferred_element_type=F32))
        ri.append(jnp.dot(xg, wx_ref[g], preferred_element_type=F32))
    r = jax.nn.sigmoid(jnp.concatenate(ra, axis=1) + ba_ref[...])
    i = jax.nn.sigmoid(jnp.concatenate(ri, axis=1) + bx_ref[...])

    log_a = (-LRU_C * jax.nn.softplus(-lam_ref[...])) * r
    a = jnp.exp(log_a)
    u = jnp.sqrt(-jnp.tanh(log_a) * (a * a + 1.0)) * i * xr

    groups = ts // SUBLANES
    a3 = a.reshape(groups, SUBLANES, d)
    h3 = u.reshape(groups, SUBLANES, d)
    sub = lax.broadcasted_iota(jnp.int32, (groups, SUBLANES, d), 1)
    step = 1
    while step < SUBLANES:
        keep = sub >= step
        a_prev = jnp.where(keep, pltpu.roll(a3, step, axis=1), 1.0)
        h_prev = jnp.where(keep, pltpu.roll(h3, step, axis=1), 0.0)
        h3 = h3 + a3 * h_prev
        a3 = a3 * a_prev
        step *= 2

    carry = hc_ref[...]
    outs = []
    for g in range(groups):
        hg = h3[g] + a3[g] * carry
        outs.append(hg)
        carry = jnp.broadcast_to(hg[SUBLANES - 1:SUBLANES, :], (SUBLANES, d))
    hc_ref[...] = carry
    h = jnp.concatenate(outs, axis=0)
    y_ref[...] = (h * gg_ref[...]).astype(BF16)


def _rglru(ux, gg, cw, cb, wa_bd, ba, wx_bd, bx, lam, batch, seq):
    t, d = ux.shape
    ts = SCAN_ROWS
    per_seq = seq // ts
    row = pl.BlockSpec((ts, d), lambda b, s: (b * per_seq + s, 0))
    return pl.pallas_call(
        functools.partial(_rglru_kernel, ts=ts, d=d),
        grid=(batch, per_seq),
        in_specs=[row, row] + [_whole()] * 7,
        out_specs=row,
        out_shape=jax.ShapeDtypeStruct((t, d), BF16),
        scratch_shapes=[pltpu.VMEM((ts + SUBLANES, d), F32),
                        pltpu.VMEM((SUBLANES, d), F32)],
        compiler_params=pltpu.CompilerParams(
            dimension_semantics=("arbitrary", "arbitrary"), vmem_limit_bytes=VMEM_LIMIT),
        name="rglru",
    )(ux, gg, cw, cb, wa_bd, ba, wx_bd, bx, lam)


def _attn_kernel(lam_ref, g_ref, qt_ref, k_ref, vt_ref, o_ref, s_ref, p_ref, *, lam_init):
    n_heads, n_tiles, feat, tq = qt_ref.shape
    lv = lam_ref[...]
    lam = (jnp.exp(jnp.sum(lv[0:1, :] * lv[1:2, :], axis=-1, keepdims=True))
           - jnp.exp(jnp.sum(lv[2:3, :] * lv[3:4, :], axis=-1, keepdims=True))
           + lam_init)
    first_sub = lax.broadcasted_iota(jnp.int32, (feat, tq), 0) < HEAD_DIM
    key_c = lax.broadcasted_iota(jnp.int32, (tq, 2 * tq), 0) // CHUNK
    qry_c = lax.broadcasted_iota(jnp.int32, (tq, 2 * tq), 1) % tq // CHUNK
    diag_ok = key_c <= qry_c

    def stacked_q(hd, i):
        qt = qt_ref[hd, i].astype(F32)
        return jnp.concatenate([jnp.where(first_sub, qt, 0.0),
                                jnp.where(first_sub, 0.0, qt)], axis=1).astype(BF16)

    def score_chunk(slot, hd, i, c, qq, mx):
        s = jnp.dot(k_ref[c * tq:(c + 1) * tq, hd * feat:(hd + 1) * feat], qq,
                    preferred_element_type=F32)
        if c == i:
            s = jnp.where(diag_ok, s, MASK_VALUE)
        s_ref[slot, c] = s
        part = jnp.max(s.reshape(tq // SUBLANES, SUBLANES, 2 * tq), axis=0)
        return part if mx is None else jnp.maximum(mx, part)

    def prob_chunk(slot, c, m, lsum):
        p = jnp.exp2(s_ref[slot, c] - m)
        p_ref[slot, c * tq:(c + 1) * tq, :] = p.astype(BF16)
        return lsum + jnp.sum(p.reshape(tq // SUBLANES, SUBLANES, 2 * tq), axis=0)

    def finish(slot, hd, i, lsum):
        k_end = (i + 1) * tq
        acc = jnp.dot(vt_ref[hd, :, 0:k_end], p_ref[slot, 0:k_end, :],
                      preferred_element_type=F32)
        acc = acc * (1.0 / jnp.sum(lsum, axis=0, keepdims=True))
        o = (acc[:, 0:tq] - lam * acc[:, tq:2 * tq]).T
        o = _rms(o, g_ref[...], SUBLN_EPS) * (1.0 - lam_init)
        o_ref[i * tq:(i + 1) * tq, hd * feat:(hd + 1) * feat] = o.astype(BF16)

    items = [(hd, i) for hd in range(n_heads) for i in range(n_tiles)]
    hd, i = items[0]
    qq = stacked_q(hd, i)
    mx = None
    for c in range(i + 1):
        mx = score_chunk(0, hd, i, c, qq, mx)
    for j, (hd, i) in enumerate(items):
        slot = j % 2
        m = jnp.max(mx, axis=0, keepdims=True)
        lsum = jnp.zeros((SUBLANES, 2 * tq), F32)
        nxt = items[j + 1] if j + 1 < len(items) else None
        n_next = nxt[1] + 1 if nxt else 0
        mx = None
        if nxt:
            qq = stacked_q(*nxt)
        for c in range(max(i + 1, n_next)):
            if c < n_next:
                mx = score_chunk(1 - slot, nxt[0], nxt[1], c, qq, mx)
            if c <= i:
                lsum = prob_chunk(slot, c, m, lsum)
        finish(slot, hd, i, lsum)


def _attention(lam_vecs, subln_g, qt, k, vt, lam_init):
    b, s, w = k.shape
    heads, n_tiles = qt.shape[1], qt.shape[2]
    hps = ATT_HEADS
    tiles = pl.BlockSpec((None, hps, n_tiles, LANES, ATT_Q), lambda bi, hi: (bi, hi, 0, 0, 0))
    cols = pl.BlockSpec((None, hps, LANES, s), lambda bi, hi: (bi, hi, 0, 0))
    rows = pl.BlockSpec((None, s, hps * LANES), lambda bi, hi: (bi, 0, hi))
    return pl.pallas_call(
        functools.partial(_attn_kernel, lam_init=lam_init),
        grid=(b, heads // hps),
        in_specs=[_whole(), _whole(), tiles, rows, cols],
        out_specs=rows,
        out_shape=jax.ShapeDtypeStruct((b, s, w), BF16),
        scratch_shapes=[pltpu.VMEM((2, n_tiles, ATT_Q, 2 * ATT_Q), F32),
                        pltpu.VMEM((2, s, 2 * ATT_Q), BF16)],
        compiler_params=pltpu.CompilerParams(
            dimension_semantics=("arbitrary", "arbitrary"), vmem_limit_bytes=VMEM_LIMIT),
        name="diffattn",
    )(lam_vecs, subln_g, qt, k, vt)


def _mlp_kernel(x_ref, yr_ref, ya_ref, sr_ref, sa_ref, wr_ref, wa_ref, wo_ref,
                g2_ref, w1_ref, w2_ref, gf_ref, o_ref, *, d, d_ff, final_norm):
    br = jnp.dot(yr_ref[...], wr_ref[...], preferred_element_type=F32)
    ba = jnp.dot(ya_ref[...], wa_ref[...], preferred_element_type=F32)
    merged = (sr_ref[...] * br + sa_ref[...] * ba).astype(BF16)
    x1 = x_ref[...] + jnp.dot(merged, wo_ref[...], preferred_element_type=F32)
    h2 = _rms(x1, g2_ref[...], NORM_EPS).astype(BF16)
    x2 = x1
    for c in range(d_ff // d):
        hid = jnp.dot(h2, w1_ref[:, c * d:(c + 1) * d], preferred_element_type=F32)
        hid = jnp.square(jnp.maximum(hid, 0.0)).astype(BF16)
        x2 = x2 + jnp.dot(hid, w2_ref[c * d:(c + 1) * d, :], preferred_element_type=F32)
    if final_norm:
        x2 = _rms(x2, gf_ref[...], NORM_EPS)
    o_ref[...] = x2


def _merge_mlp(x2, yr, ya, sr, sa, wr, wa, wo, g2, w1, w2, gf, final_norm):
    t, d = x2.shape
    d_ff = w1.shape[1]
    tm = MLP_ROWS
    row = pl.BlockSpec((tm, d), lambda i: (i, 0))
    return pl.pallas_call(
        functools.partial(_mlp_kernel, d=d, d_ff=d_ff, final_norm=final_norm),
        grid=(t // tm,),
        in_specs=[row] * 5 + [_whole()] * 7,
        out_specs=row,
        out_shape=jax.ShapeDtypeStruct((t, d), F32),
        compiler_params=pltpu.CompilerParams(
            dimension_semantics=("arbitrary",), vmem_limit_bytes=VMEM_LIMIT),
        name="merge_mlp",
    )(x2, yr, ya, sr, sa, wr, wa, wo, g2, w1, w2, gf)


def _rope_tables(seq):
    half = HEAD_DIM // 2
    inv_freq = ROPE_THETA ** (-jnp.arange(half, dtype=F32) * 2.0 / HEAD_DIM)
    ang = jnp.arange(seq, dtype=F32)[:, None] * inv_freq[None, :]
    cos, sin = jnp.cos(ang), jnp.sin(ang)
    reps = LANES // HEAD_DIM
    cos_t = jnp.tile(jnp.concatenate([cos, cos], axis=1), (1, reps))
    sin_t = jnp.tile(jnp.concatenate([-sin, sin], axis=1), (1, reps))
    return cos_t, sin_t


def _block_diag(w):
    n, bw, _ = w.shape
    per = MXU_DIM // bw
    w = w.reshape(n // per, per, bw, bw)
    eye = jnp.eye(per, dtype=w.dtype)
    return jnp.einsum('gpij,pq->gpiqj', w, eye).reshape(n // per, MXU_DIM, MXU_DIM)


def kernel(x, norm1_g, w_in, conv_w, conv_b, rg_a_w, rg_a_b, rg_x_w, rg_x_b, lru_lambda,
           lambda_q1, lambda_k1, lambda_q2, lambda_k2, subln_g, w_br_rnn, w_br_attn,
           w_out, norm2_g, w_mlp1, w_mlp2, normf_g):
    bsz, seq, d = x.shape
    depth = w_in.shape[0]
    t = bsz * seq
    cos_t, sin_t = _rope_tables(seq)
    xf = x.reshape(t, d)
    for l in range(depth):
        lam_init = 0.8 - 0.6 * math.exp(-0.3 * l)
        ux, gg, qt, k, vt, sr, sa = _inproj(
            xf, norm1_g[l].reshape(1, d), w_in[l].astype(BF16), cos_t, sin_t, bsz, seq)
        y_rnn = _rglru(
            ux, gg, conv_w[l], conv_b[l].reshape(1, d),
            _block_diag(rg_a_w[l]).astype(BF16), rg_a_b[l].reshape(1, d),
            _block_diag(rg_x_w[l]).astype(BF16), rg_x_b[l].reshape(1, d),
            lru_lambda[l].reshape(1, d), bsz, seq)
        lam_vecs = jnp.stack([lambda_q1[l], lambda_k1[l], lambda_q2[l], lambda_k2[l]])
        y_attn = _attention(
            lam_vecs.astype(F32), subln_g[l].reshape(1, LANES),
            qt, k.reshape(bsz, seq, d), vt, lam_init)
        xf = _merge_mlp(
            xf, y_rnn, y_attn.reshape(t, d), sr, sa,
            w_br_rnn[l].astype(BF16), w_br_attn[l].astype(BF16), w_out[l].astype(BF16),
            norm2_g[l].reshape(1, d), w_mlp1[l].astype(BF16), w_mlp2[l].astype(BF16),
            normf_g.reshape(1, d), final_norm=(l == depth - 1))
    return xf.reshape(bsz, seq, d)
```

```python
import functools
import math

import jax
import jax.numpy as jnp
from jax import lax
from jax.experimental import pallas as pl
from jax.experimental.pallas import tpu as pltpu

F32 = jnp.float32
BF16 = jnp.bfloat16

CHUNK = 64
CONV_W = 4
LRU_C = 8.0
HEAD_DIM = 64
ROPE_THETA = 10000.0
NORM_EPS = 1e-6
SUBLN_EPS = 1e-5
MASK_VALUE = -1e30
F32_TINY = 1.1754944e-38
LOG2_E = 1.4426950408889634

LANES = 128
SUBLANES = 8
MXU_DIM = 256
VMEM_LIMIT = 56 * 1024 * 1024

PROJ_ROWS = 512
ATT_Q = 256
ATT_HEADS = 1
MLP_ROWS = 256


def _rms(x, g, eps):
    ms = jnp.mean(x * x, axis=-1, keepdims=True)
    return x * lax.rsqrt(ms + eps) * g


def _sigmoid(x):
    return 0.5 * jnp.tanh(0.5 * x) + 0.5


def _whole(memory_space=pltpu.VMEM):
    return pl.BlockSpec(memory_space=memory_space)


def _rope_cols(y, cos, sin_signed, first_half):
    outs = []
    for c in range(y.shape[1] // LANES):
        blk = y[:, c * LANES:(c + 1) * LANES]
        rot = jnp.where(first_half,
                        pltpu.roll(blk, LANES - HEAD_DIM // 2, axis=1),
                        pltpu.roll(blk, HEAD_DIM // 2, axis=1))
        outs.append(blk * cos + rot * sin_signed)
    return jnp.concatenate(outs, axis=1)


def _store_head_tiles(ref, y):
    yt = y.T
    for hd in range(ref.shape[0]):
        for j in range(ref.shape[1]):
            ref[hd, j] = yt[hd * LANES:(hd + 1) * LANES,
                            j * ATT_Q:(j + 1) * ATT_Q].astype(BF16)


def _store_head_rows(ref, y):
    yt = y.T
    for hd in range(ref.shape[0]):
        ref[hd] = yt[hd * LANES:(hd + 1) * LANES, :].astype(BF16)


def _lru_inputs_group(g, ts, ext_ref, cw_ref, cb_ref, wa_ref, ba_ref, wx_ref, bx_ref, lam_ref):
    cs = slice(g * MXU_DIM, (g + 1) * MXU_DIM)
    cw = cw_ref[:, cs]
    xr = ext_ref[SUBLANES - 3:SUBLANES - 3 + ts, cs] * cw[0:1, :]
    for j in range(1, CONV_W):
        o = SUBLANES - (CONV_W - 1) + j
        xr = xr + ext_ref[o:o + ts, cs] * cw[j:j + 1, :]
    xr = xr + cb_ref[:, cs]

    xb = xr.astype(BF16)
    r = _sigmoid(jnp.dot(xb, wa_ref[g], preferred_element_type=F32) + ba_ref[:, cs])
    i = _sigmoid(jnp.dot(xb, wx_ref[g], preferred_element_type=F32) + bx_ref[:, cs])

    log_a = (-LRU_C * jax.nn.softplus(-lam_ref[:, cs])) * r
    a = jnp.exp(log_a)
    y = -jnp.tanh(log_a) * (a * a + 1.0)
    u = y * lax.rsqrt(jnp.maximum(y, F32_TINY)) * i * xr
    return a, u


def _inproj_kernel(x_ref, g_ref, w_ref, cos_ref, sin_ref,
                   cw_ref, cb_ref, wa_ref, ba_ref, wx_ref, bx_ref, lam_ref,
                   a_ref, u_ref, ug_ref, qt_ref, k_ref, vt_ref, gr_ref, ga_ref, ext_ref,
                   *, d, per_seq):
    ts = x_ref.shape[0]

    @pl.when(pl.program_id(0) % per_seq == 0)
    def _():
        ext_ref[0:SUBLANES, :] = jnp.zeros((SUBLANES, d), F32)

    h = _rms(x_ref[...], g_ref[...], NORM_EPS).astype(BF16)

    def proj(seg):
        return jnp.dot(h, w_ref[:, seg * d:(seg + 1) * d], preferred_element_type=F32)

    def lru_inputs(g):
        cs = slice(g * MXU_DIM, (g + 1) * MXU_DIM)
        a_ref[:, cs], u_ref[:, cs] = _lru_inputs_group(
            g, ts, ext_ref, cw_ref, cb_ref, wa_ref, ba_ref, wx_ref, bx_ref, lam_ref)

    cos = cos_ref[...]
    sin_signed = sin_ref[...]
    lane = lax.broadcasted_iota(jnp.int32, cos.shape, 1)
    first_half = (lane % HEAD_DIM) < (HEAD_DIM // 2)

    def ux_group(g):
        cs = slice(g * MXU_DIM, (g + 1) * MXU_DIM)
        ext_ref[SUBLANES:, cs] = jnp.dot(h, w_ref[:, cs], preferred_element_type=F32)

    ux_group(0)
    ug_ref[...] = proj(1)
    ux_group(1)
    lru_inputs(0)
    _store_head_tiles(qt_ref, _rope_cols(proj(2), cos, sin_signed, first_half)
                      * (HEAD_DIM ** -0.5 * LOG2_E))
    ux_group(2)
    lru_inputs(1)
    k_ref[...] = _rope_cols(proj(3), cos, sin_signed, first_half).astype(BF16)
    ux_group(3)
    lru_inputs(2)
    _store_head_rows(vt_ref, proj(4))
    lru_inputs(3)
    ext_ref[0:SUBLANES, :] = ext_ref[ts:ts + SUBLANES, :]
    gr_ref[...] = proj(5)
    ga_ref[...] = proj(6)


def _inproj(x2, g, w_bf, cos_t, sin_t, cw, cb, wa_bd, ba, wx_bd, bx, lam, batch, seq):
    t, d = x2.shape
    tm = PROJ_ROWS
    per_seq = seq // tm
    heads = d // LANES
    row = pl.BlockSpec((tm, d), lambda i: (i, 0))
    tab = pl.BlockSpec((tm, LANES), lambda i: (i % per_seq, 0))
    tiles = pl.BlockSpec((None, heads, tm // ATT_Q, LANES, ATT_Q),
                         lambda i: (i // per_seq, 0, i % per_seq, 0, 0))
    cols = pl.BlockSpec((None, heads, LANES, tm), lambda i: (i // per_seq, 0, 0, i % per_seq))
    f32o = jax.ShapeDtypeStruct((t, d), F32)
    bf16o = jax.ShapeDtypeStruct((t, d), BF16)
    tileo = jax.ShapeDtypeStruct((batch, heads, seq // ATT_Q, LANES, ATT_Q), BF16)
    colo = jax.ShapeDtypeStruct((batch, heads, LANES, seq), BF16)
    return pl.pallas_call(
        functools.partial(_inproj_kernel, d=d, per_seq=per_seq),
        grid=(t // tm,),
        in_specs=[row, _whole(), _whole(), tab, tab] + [_whole()] * 7,
        out_specs=[row, row, row, tiles, row, cols, row, row],
        out_shape=[f32o, f32o, f32o, tileo, bf16o, colo, f32o, f32o],
        scratch_shapes=[pltpu.VMEM((tm + SUBLANES, d), F32)],
        compiler_params=pltpu.CompilerParams(
            dimension_semantics=("arbitrary",), vmem_limit_bytes=VMEM_LIMIT),
        name="inproj",
    )(x2, g, w_bf, cos_t, sin_t, cw, cb, wa_bd, ba, wx_bd, bx, lam)


def _lru_scan_group(g, ts, a_ref, u_ref, ug_ref, hc_ref):
    w = MXU_DIM
    cs = slice(g * w, (g + 1) * w)
    groups = ts // SUBLANES
    a3 = a_ref[:, cs].reshape(groups, SUBLANES, w)
    h3 = u_ref[:, cs].reshape(groups, SUBLANES, w)
    sub = lax.broadcasted_iota(jnp.int32, (groups, SUBLANES, w), 1)
    step = 1
    while step < SUBLANES:
        keep = sub >= step
        a_prev = jnp.where(keep, pltpu.roll(a3, step, axis=1), 1.0)
        h_prev = jnp.where(keep, pltpu.roll(h3, step, axis=1), 0.0)
        h3 = h3 + a3 * h_prev
        a3 = a3 * a_prev
        step *= 2

    carry = hc_ref[:, cs]
    outs = []
    for q in range(groups):
        hq = h3[q] + a3[q] * carry
        outs.append(hq)
        carry = jnp.broadcast_to(hq[SUBLANES - 1:SUBLANES, :], (SUBLANES, w))
    hc_ref[:, cs] = carry
    return (jnp.concatenate(outs, axis=0) * jax.nn.gelu(ug_ref[:, cs])).astype(BF16)


def _attn_kernel(lam_ref, g_ref, qt_ref, k_ref, vt_ref, o_ref, s_ref, p_ref, *, lam_init):
    n_heads, n_tiles, feat, tq = qt_ref.shape
    lv = lam_ref[...]
    lam = (jnp.exp(jnp.sum(lv[0:1, :] * lv[1:2, :], axis=-1, keepdims=True))
           - jnp.exp(jnp.sum(lv[2:3, :] * lv[3:4, :], axis=-1, keepdims=True))
           + lam_init)
    first_sub = lax.broadcasted_iota(jnp.int32, (feat, tq), 0) < HEAD_DIM
    key_c = lax.broadcasted_iota(jnp.int32, (tq, 2 * tq), 0) // CHUNK
    qry_c = lax.broadcasted_iota(jnp.int32, (tq, 2 * tq), 1) % tq // CHUNK
    diag_ok = key_c <= qry_c

    def stacked_q(hd, i):
        qt = qt_ref[hd, i].astype(F32)
        return jnp.concatenate([jnp.where(first_sub, qt, 0.0),
                                jnp.where(first_sub, 0.0, qt)], axis=1).astype(BF16)

    def score_chunk(slot, hd, i, c, qq, mx):
        s = jnp.dot(k_ref[c * tq:(c + 1) * tq, hd * feat:(hd + 1) * feat], qq,
                    preferred_element_type=F32)
        if c == i:
            s = jnp.where(diag_ok, s, MASK_VALUE)
        s_ref[slot, c] = s
        part = jnp.max(s.reshape(tq // SUBLANES, SUBLANES, 2 * tq), axis=0)
        return part if mx is None else jnp.maximum(mx, part)

    def prob_chunk(slot, c, m, lsum):
        p = jnp.exp2(s_ref[slot, c] - m)
        p_ref[slot, c * tq:(c + 1) * tq, :] = p.astype(BF16)
        return lsum + jnp.sum(p.reshape(tq // SUBLANES, SUBLANES, 2 * tq), axis=0)

    def finish(slot, hd, i, lsum):
        k_end = (i + 1) * tq
        acc = jnp.dot(vt_ref[hd, :, 0:k_end], p_ref[slot, 0:k_end, :],
                      preferred_element_type=F32)
        acc = acc * (1.0 / jnp.sum(lsum, axis=0, keepdims=True))
        o = (acc[:, 0:tq] - lam * acc[:, tq:2 * tq]).T
        o = _rms(o, g_ref[...], SUBLN_EPS) * (1.0 - lam_init)
        o_ref[i * tq:(i + 1) * tq, hd * feat:(hd + 1) * feat] = o.astype(BF16)

    items = [(hd, i) for hd in range(n_heads) for i in range(n_tiles)]
    hd, i = items[0]
    qq = stacked_q(hd, i)
    mx = None
    for c in range(i + 1):
        mx = score_chunk(0, hd, i, c, qq, mx)
    for j, (hd, i) in enumerate(items):
        slot = j % 2
        m = jnp.max(mx, axis=0, keepdims=True)
        lsum = jnp.zeros((SUBLANES, 2 * tq), F32)
        nxt = items[j + 1] if j + 1 < len(items) else None
        n_next = nxt[1] + 1 if nxt else 0
        mx = None
        if nxt:
            qq = stacked_q(*nxt)
        for c in range(max(i + 1, n_next)):
            if c < n_next:
                mx = score_chunk(1 - slot, nxt[0], nxt[1], c, qq, mx)
            if c <= i:
                lsum = prob_chunk(slot, c, m, lsum)
        finish(slot, hd, i, lsum)


def _attention(lam_vecs, subln_g, qt, k, vt, lam_init):
    b, s, w = k.shape
    heads, n_tiles = qt.shape[1], qt.shape[2]
    hps = ATT_HEADS
    tiles = pl.BlockSpec((None, hps, n_tiles, LANES, ATT_Q), lambda bi, hi: (bi, hi, 0, 0, 0))
    cols = pl.BlockSpec((None, hps, LANES, s), lambda bi, hi: (bi, hi, 0, 0))
    rows = pl.BlockSpec((None, s, hps * LANES), lambda bi, hi: (bi, 0, hi))
    return pl.pallas_call(
        functools.partial(_attn_kernel, lam_init=lam_init),
        grid=(b, heads // hps),
        in_specs=[_whole(), _whole(), tiles, rows, cols],
        out_specs=rows,
        out_shape=jax.ShapeDtypeStruct((b, s, w), BF16),
        scratch_shapes=[pltpu.VMEM((2, n_tiles, ATT_Q, 2 * ATT_Q), F32),
                        pltpu.VMEM((2, s, 2 * ATT_Q), BF16)],
        compiler_params=pltpu.CompilerParams(
            dimension_semantics=("arbitrary", "arbitrary"), vmem_limit_bytes=VMEM_LIMIT),
        name="diffattn",
    )(lam_vecs, subln_g, qt, k, vt)


def _mlp_kernel(x_ref, ya_ref, gr_ref, ga_ref, a_ref, u_ref, ug_ref,
                wr_ref, wa_ref, wo_ref, g2_ref, w1_ref, w2_ref, gf_ref,
                o_ref, y_ref, hc_ref, *, d, d_ff, n_row_tiles, per_seq, final_norm):
    i = pl.program_id(0)
    ts = x_ref.shape[0]
    scan_tile = jnp.minimum(i, n_row_tiles - 1)

    @pl.when(scan_tile % per_seq == 0)
    def _():
        hc_ref[...] = jnp.zeros((SUBLANES, d), F32)

    @pl.when(i == 0)
    def _():
        y_ref[1] = jnp.zeros(y_ref.shape[1:], BF16)

    def scan_group(g):
        y_ref[i % 2, :, g * MXU_DIM:(g + 1) * MXU_DIM] = _lru_scan_group(
            g, ts, a_ref, u_ref, ug_ref, hc_ref)

    n_groups = d // MXU_DIM
    n_chunks = d_ff // d
    br = jnp.dot(y_ref[(i + 1) % 2], wr_ref[...], preferred_element_type=F32)
    ba = jnp.dot(ya_ref[...], wa_ref[...], preferred_element_type=F32)
    merged = (_sigmoid(gr_ref[...]) * br + _sigmoid(ga_ref[...]) * ba).astype(BF16)
    x1 = x_ref[...] + jnp.dot(merged, wo_ref[...], preferred_element_type=F32)
    h2 = _rms(x1, g2_ref[...], NORM_EPS).astype(BF16)
    x2 = x1
    for c in range(max(n_chunks, n_groups)):
        if c < n_groups:
            scan_group(c)
        if c < n_chunks:
            hid = jnp.dot(h2, w1_ref[:, c * d:(c + 1) * d], preferred_element_type=F32)
            hid = jnp.square(jnp.maximum(hid, 0.0)).astype(BF16)
            x2 = x2 + jnp.dot(hid, w2_ref[c * d:(c + 1) * d, :], preferred_element_type=F32)
    if final_norm:
        x2 = _rms(x2, gf_ref[...], NORM_EPS)
    o_ref[...] = x2


def _merge_mlp(x2, ya, gr, ga, a, u, ug, wr, wa, wo, g2, w1, w2, gf, seq, final_norm):
    t, d = x2.shape
    d_ff = w1.shape[1]
    tm = MLP_ROWS
    n = t // tm
    mlp_row = pl.BlockSpec((tm, d), lambda i: (jnp.maximum(i - 1, 0), 0))
    scan_row = pl.BlockSpec((tm, d), lambda i: (jnp.minimum(i, n - 1), 0))
    return pl.pallas_call(
        functools.partial(_mlp_kernel, d=d, d_ff=d_ff, n_row_tiles=n, per_seq=seq // tm,
                          final_norm=final_norm),
        grid=(n + 1,),
        in_specs=[mlp_row] * 4 + [scan_row] * 3 + [_whole()] * 7,
        out_specs=mlp_row,
        out_shape=jax.ShapeDtypeStruct((t, d), F32),
        scratch_shapes=[pltpu.VMEM((2, tm, d), BF16),
                        pltpu.VMEM((SUBLANES, d), F32)],
        compiler_params=pltpu.CompilerParams(
            dimension_semantics=("arbitrary",), vmem_limit_bytes=VMEM_LIMIT),
        name="merge_mlp",
    )(x2, ya, gr, ga, a, u, ug, wr, wa, wo, g2, w1, w2, gf)


def _rope_tables(seq):
    half = HEAD_DIM // 2
    inv_freq = ROPE_THETA ** (-jnp.arange(half, dtype=F32) * 2.0 / HEAD_DIM)
    ang = jnp.arange(seq, dtype=F32)[:, None] * inv_freq[None, :]
    cos, sin = jnp.cos(ang), jnp.sin(ang)
    reps = LANES // HEAD_DIM
    cos_t = jnp.tile(jnp.concatenate([cos, cos], axis=1), (1, reps))
    sin_t = jnp.tile(jnp.concatenate([-sin, sin], axis=1), (1, reps))
    return cos_t, sin_t


def _block_diag(w):
    n, bw, _ = w.shape
    per = MXU_DIM // bw
    w = w.reshape(n // per, per, bw, bw)
    eye = jnp.eye(per, dtype=w.dtype)
    return jnp.einsum('gpij,pq->gpiqj', w, eye).reshape(n // per, MXU_DIM, MXU_DIM)


def kernel(x, norm1_g, w_in, conv_w, conv_b, rg_a_w, rg_a_b, rg_x_w, rg_x_b, lru_lambda,
           lambda_q1, lambda_k1, lambda_q2, lambda_k2, subln_g, w_br_rnn, w_br_attn,
           w_out, norm2_g, w_mlp1, w_mlp2, normf_g):
    bsz, seq, d = x.shape
    depth = w_in.shape[0]
    t = bsz * seq
    cos_t, sin_t = _rope_tables(seq)
    xf = x.reshape(t, d)
    for l in range(depth):
        lam_init = 0.8 - 0.6 * math.exp(-0.3 * l)
        lru_a, lru_u, ug, qt, k, vt, gr, ga = _inproj(
            xf, norm1_g[l].reshape(1, d), w_in[l].astype(BF16), cos_t, sin_t,
            conv_w[l], conv_b[l].reshape(1, d),
            _block_diag(rg_a_w[l]).astype(BF16), rg_a_b[l].reshape(1, d),
            _block_diag(rg_x_w[l]).astype(BF16), rg_x_b[l].reshape(1, d),
            lru_lambda[l].reshape(1, d), bsz, seq)
        lam_vecs = jnp.stack([lambda_q1[l], lambda_k1[l], lambda_q2[l], lambda_k2[l]])
        y_attn = _attention(
            lam_vecs.astype(F32), subln_g[l].reshape(1, LANES),
            qt, k.reshape(bsz, seq, d), vt, lam_init)
        xf = _merge_mlp(
            xf, y_attn.reshape(t, d), gr, ga, lru_a, lru_u, ug,
            w_br_rnn[l].astype(BF16), w_br_attn[l].astype(BF16), w_out[l].astype(BF16),
            norm2_g[l].reshape(1, d), w_mlp1[l].astype(BF16), w_mlp2[l].astype(BF16),
            normf_g.reshape(1, d), seq, final_norm=(l == depth - 1))
    return xf.reshape(bsz, seq, d)
```

```python
import functools
import math

import jax
import jax.numpy as jnp
from jax import lax
from jax.experimental import pallas as pl
from jax.experimental.pallas import tpu as pltpu

F32 = jnp.float32
BF16 = jnp.bfloat16

CHUNK = 64
CONV_W = 4
LRU_C = 8.0
HEAD_DIM = 64
ROPE_THETA = 10000.0
NORM_EPS = 1e-6
SUBLN_EPS = 1e-5
MASK_VALUE = -1e30
F32_TINY = 1.1754944e-38
LOG2_E = 1.4426950408889634

LANES = 128
SUBLANES = 8
MXU_DIM = 256
VMEM_LIMIT = 56 * 1024 * 1024

PROJ_ROWS = 512
ATT_Q = 256
ATT_HEADS = 2
MLP_ROWS = 256


def _rms(x, g, eps):
    ms = jnp.mean(x * x, axis=-1, keepdims=True)
    return x * lax.rsqrt(ms + eps) * g


def _sigmoid(x):
    return 0.5 * jnp.tanh(0.5 * x) + 0.5


def _whole(memory_space=pltpu.VMEM):
    return pl.BlockSpec(memory_space=memory_space)


def _rope_cols(y, cos, sin_signed, first_half):
    outs = []
    for c in range(y.shape[1] // LANES):
        blk = y[:, c * LANES:(c + 1) * LANES]
        rot = jnp.where(first_half,
                        pltpu.roll(blk, LANES - HEAD_DIM // 2, axis=1),
                        pltpu.roll(blk, HEAD_DIM // 2, axis=1))
        outs.append(blk * cos + rot * sin_signed)
    return jnp.concatenate(outs, axis=1)


def _store_head_tiles(ref, y):
    yt = y.T
    for hd in range(ref.shape[0]):
        for j in range(ref.shape[1]):
            ref[hd, j] = yt[hd * LANES:(hd + 1) * LANES,
                            j * ATT_Q:(j + 1) * ATT_Q].astype(BF16)


def _store_head_rows(ref, y):
    yt = y.T
    for hd in range(ref.shape[0]):
        ref[hd] = yt[hd * LANES:(hd + 1) * LANES, :].astype(BF16)


def _lru_inputs_group(g, ts, ext_ref, cw_ref, cb_ref, wa_ref, ba_ref, wx_ref, bx_ref, lam_ref):
    cs = slice(g * MXU_DIM, (g + 1) * MXU_DIM)
    cw = cw_ref[:, cs]
    xr = ext_ref[SUBLANES - 3:SUBLANES - 3 + ts, cs] * cw[0:1, :]
    for j in range(1, CONV_W):
        o = SUBLANES - (CONV_W - 1) + j
        xr = xr + ext_ref[o:o + ts, cs] * cw[j:j + 1, :]
    xr = xr + cb_ref[:, cs]

    xb = xr.astype(BF16)
    r = _sigmoid(jnp.dot(xb, wa_ref[g], preferred_element_type=F32) + ba_ref[:, cs])
    i = _sigmoid(jnp.dot(xb, wx_ref[g], preferred_element_type=F32) + bx_ref[:, cs])

    log_a = (-LRU_C * jax.nn.softplus(-lam_ref[:, cs])) * r
    a = jnp.exp(log_a)
    y = -jnp.tanh(log_a) * (a * a + 1.0)
    u = y * lax.rsqrt(jnp.maximum(y, F32_TINY)) * i * xr
    return a, u


def _inproj_kernel(x_ref, g_ref, w_ref, cos_ref, sin_ref,
                   cw_ref, cb_ref, wa_ref, ba_ref, wx_ref, bx_ref, lam_ref,
                   a_ref, u_ref, ug_ref, qt_ref, k_ref, vt_ref, gr_ref, ga_ref, ext_ref,
                   *, d, per_seq):
    ts = x_ref.shape[0]

    @pl.when(pl.program_id(0) % per_seq == 0)
    def _():
        ext_ref[0:SUBLANES, :] = jnp.zeros((SUBLANES, d), F32)

    h = _rms(x_ref[...], g_ref[...], NORM_EPS).astype(BF16)

    def proj(seg):
        return jnp.dot(h, w_ref[:, seg * d:(seg + 1) * d], preferred_element_type=F32)

    def lru_inputs(g):
        cs = slice(g * MXU_DIM, (g + 1) * MXU_DIM)
        a, u = _lru_inputs_group(
            g, ts, ext_ref, cw_ref, cb_ref, wa_ref, ba_ref, wx_ref, bx_ref, lam_ref)
        a_ref[:, cs] = a
        u_ref[:, cs] = u.astype(BF16)

    cos = cos_ref[...]
    sin_signed = sin_ref[...]
    lane = lax.broadcasted_iota(jnp.int32, cos.shape, 1)
    first_half = (lane % HEAD_DIM) < (HEAD_DIM // 2)

    def ux_group(g):
        cs = slice(g * MXU_DIM, (g + 1) * MXU_DIM)
        ext_ref[SUBLANES:, cs] = jnp.dot(h, w_ref[:, cs], preferred_element_type=F32)

    ux_group(0)
    ug_ref[...] = proj(1).astype(BF16)
    ux_group(1)
    lru_inputs(0)
    _store_head_tiles(qt_ref, _rope_cols(proj(2), cos, sin_signed, first_half)
                      * (HEAD_DIM ** -0.5 * LOG2_E))
    ux_group(2)
    lru_inputs(1)
    k_ref[...] = _rope_cols(proj(3), cos, sin_signed, first_half).astype(BF16)
    ux_group(3)
    lru_inputs(2)
    _store_head_rows(vt_ref, proj(4))
    lru_inputs(3)
    ext_ref[0:SUBLANES, :] = ext_ref[ts:ts + SUBLANES, :]
    gr_ref[...] = proj(5).astype(BF16)
    ga_ref[...] = proj(6).astype(BF16)


def _inproj(x2, g, w_bf, cos_t, sin_t, cw, cb, wa_bd, ba, wx_bd, bx, lam, batch, seq):
    t, d = x2.shape
    tm = PROJ_ROWS
    per_seq = seq // tm
    heads = d // LANES
    row = pl.BlockSpec((tm, d), lambda i: (i, 0))
    tab = pl.BlockSpec((tm, LANES), lambda i: (i % per_seq, 0))
    tiles = pl.BlockSpec((None, heads, tm // ATT_Q, LANES, ATT_Q),
                         lambda i: (i // per_seq, 0, i % per_seq, 0, 0))
    cols = pl.BlockSpec((None, heads, LANES, tm), lambda i: (i // per_seq, 0, 0, i % per_seq))
    f32o = jax.ShapeDtypeStruct((t, d), F32)
    bf16o = jax.ShapeDtypeStruct((t, d), BF16)
    tileo = jax.ShapeDtypeStruct((batch, heads, seq // ATT_Q, LANES, ATT_Q), BF16)
    colo = jax.ShapeDtypeStruct((batch, heads, LANES, seq), BF16)
    return pl.pallas_call(
        functools.partial(_inproj_kernel, d=d, per_seq=per_seq),
        grid=(t // tm,),
        in_specs=[row, _whole(), _whole(), tab, tab] + [_whole()] * 7,
        out_specs=[row, row, row, tiles, row, cols, row, row],
        out_shape=[f32o, bf16o, bf16o, tileo, bf16o, colo, bf16o, bf16o],
        scratch_shapes=[pltpu.VMEM((tm + SUBLANES, d), F32)],
        compiler_params=pltpu.CompilerParams(
            dimension_semantics=("arbitrary",), vmem_limit_bytes=VMEM_LIMIT),
        name="inproj",
    )(x2, g, w_bf, cos_t, sin_t, cw, cb, wa_bd, ba, wx_bd, bx, lam)


def _lru_scan_group(g, ts, a_ref, u_ref, ug_ref, hc_ref):
    w = MXU_DIM
    cs = slice(g * w, (g + 1) * w)
    groups = ts // SUBLANES
    a3 = a_ref[:, cs].reshape(groups, SUBLANES, w)
    h3 = u_ref[:, cs].astype(F32).reshape(groups, SUBLANES, w)
    sub = lax.broadcasted_iota(jnp.int32, (groups, SUBLANES, w), 1)
    step = 1
    while step < SUBLANES:
        keep = sub >= step
        a_prev = jnp.where(keep, pltpu.roll(a3, step, axis=1), 1.0)
        h_prev = jnp.where(keep, pltpu.roll(h3, step, axis=1), 0.0)
        h3 = h3 + a3 * h_prev
        a3 = a3 * a_prev
        step *= 2

    carry = hc_ref[:, cs]
    outs = []
    for q in range(groups):
        hq = h3[q] + a3[q] * carry
        outs.append(hq)
        carry = jnp.broadcast_to(hq[SUBLANES - 1:SUBLANES, :], (SUBLANES, w))
    hc_ref[:, cs] = carry
    gate = jax.nn.gelu(ug_ref[:, cs].astype(F32))
    return (jnp.concatenate(outs, axis=0) * gate).astype(BF16)


def _attn_kernel(lam_ref, g_ref, qt_ref, k_ref, vt_ref, o_ref, s_ref, p_ref, *, lam_init):
    n_heads, n_tiles, feat, tq = qt_ref.shape
    lv = lam_ref[...]
    lam = (jnp.exp(jnp.sum(lv[0:1, :] * lv[1:2, :], axis=-1, keepdims=True))
           - jnp.exp(jnp.sum(lv[2:3, :] * lv[3:4, :], axis=-1, keepdims=True))
           + lam_init)
    first_sub = lax.broadcasted_iota(jnp.int32, (feat, tq), 0) < HEAD_DIM
    key_c = lax.broadcasted_iota(jnp.int32, (tq, 2 * tq), 0) // CHUNK
    qry_c = lax.broadcasted_iota(jnp.int32, (tq, 2 * tq), 1) % tq // CHUNK
    diag_ok = key_c <= qry_c

    def stacked_q(hd, i):
        qt = qt_ref[hd, i].astype(F32)
        return jnp.concatenate([jnp.where(first_sub, qt, 0.0),
                                jnp.where(first_sub, 0.0, qt)], axis=1).astype(BF16)

    def score_chunk(slot, hd, i, c, qq, mx):
        s = jnp.dot(k_ref[c * tq:(c + 1) * tq, hd * feat:(hd + 1) * feat], qq,
                    preferred_element_type=F32)
        if c == i:
            s = jnp.where(diag_ok, s, MASK_VALUE)
        s_ref[slot, c] = s
        part = jnp.max(s.reshape(tq // SUBLANES, SUBLANES, 2 * tq), axis=0)
        return part if mx is None else jnp.maximum(mx, part)

    def prob_chunk(slot, c, m, lsum):
        p = jnp.exp2(s_ref[slot, c] - m)
        p_ref[slot, c * tq:(c + 1) * tq, :] = p.astype(BF16)
        return lsum + jnp.sum(p.reshape(tq // SUBLANES, SUBLANES, 2 * tq), axis=0)

    def finish(slot, hd, i, lsum):
        k_end = (i + 1) * tq
        acc = jnp.dot(vt_ref[hd, :, 0:k_end], p_ref[slot, 0:k_end, :],
                      preferred_element_type=F32)
        acc = acc * (1.0 / jnp.sum(lsum, axis=0, keepdims=True))
        o = (acc[:, 0:tq] - lam * acc[:, tq:2 * tq]).T
        o = _rms(o, g_ref[...], SUBLN_EPS) * (1.0 - lam_init)
        o_ref[i * tq:(i + 1) * tq, hd * feat:(hd + 1) * feat] = o.astype(BF16)

    order = list(range(0, n_tiles, 2)) + list(range(n_tiles - 1 - n_tiles % 2, 0, -2))
    items = [(hd, i) for hd in range(n_heads) for i in order]
    hd, i = items[0]
    qq = stacked_q(hd, i)
    mx = None
    for c in range(i + 1):
        mx = score_chunk(0, hd, i, c, qq, mx)
    for j, (hd, i) in enumerate(items):
        slot = j % 2
        m = jnp.max(mx, axis=0, keepdims=True)
        lsum = jnp.zeros((SUBLANES, 2 * tq), F32)
        nxt = items[j + 1] if j + 1 < len(items) else None
        n_next = nxt[1] + 1 if nxt else 0
        mx = None
        if nxt:
            qq = stacked_q(*nxt)
        for c in range(max(i + 1, n_next)):
            if c < n_next:
                mx = score_chunk(1 - slot, nxt[0], nxt[1], c, qq, mx)
            if c <= i:
                lsum = prob_chunk(slot, c, m, lsum)
        finish(slot, hd, i, lsum)


def _attention(lam_vecs, subln_g, qt, k, vt, lam_init):
    b, s, w = k.shape
    heads, n_tiles = qt.shape[1], qt.shape[2]
    hps = ATT_HEADS
    tiles = pl.BlockSpec((None, hps, n_tiles, LANES, ATT_Q), lambda bi, hi: (bi, hi, 0, 0, 0))
    cols = pl.BlockSpec((None, hps, LANES, s), lambda bi, hi: (bi, hi, 0, 0))
    rows = pl.BlockSpec((None, s, hps * LANES), lambda bi, hi: (bi, 0, hi))
    return pl.pallas_call(
        functools.partial(_attn_kernel, lam_init=lam_init),
        grid=(b, heads // hps),
        in_specs=[_whole(), _whole(), tiles, rows, cols],
        out_specs=rows,
        out_shape=jax.ShapeDtypeStruct((b, s, w), BF16),
        scratch_shapes=[pltpu.VMEM((2, n_tiles, ATT_Q, 2 * ATT_Q), F32),
                        pltpu.VMEM((2, s, 2 * ATT_Q), BF16)],
        compiler_params=pltpu.CompilerParams(
            dimension_semantics=("arbitrary", "arbitrary"), vmem_limit_bytes=VMEM_LIMIT),
        name="diffattn",
    )(lam_vecs, subln_g, qt, k, vt)


def _mlp_kernel(x_ref, ya_ref, gr_ref, ga_ref, a_ref, u_ref, ug_ref,
                wr_ref, wa_ref, wo_ref, g2_ref, w1_ref, w2_ref, gf_ref,
                o_ref, y_ref, hc_ref, *, d, d_ff, n_row_tiles, per_seq, final_norm):
    i = pl.program_id(0)
    ts = x_ref.shape[0]
    scan_tile = jnp.minimum(i, n_row_tiles - 1)

    @pl.when(scan_tile % per_seq == 0)
    def _():
        hc_ref[...] = jnp.zeros((SUBLANES, d), F32)

    @pl.when(i == 0)
    def _():
        y_ref[1] = jnp.zeros(y_ref.shape[1:], BF16)

    def scan_group(g):
        y_ref[i % 2, :, g * MXU_DIM:(g + 1) * MXU_DIM] = _lru_scan_group(
            g, ts, a_ref, u_ref, ug_ref, hc_ref)

    n_groups = d // MXU_DIM
    n_chunks = d_ff // d
    br = jnp.dot(y_ref[(i + 1) % 2], wr_ref[...], preferred_element_type=F32)
    ba = jnp.dot(ya_ref[...], wa_ref[...], preferred_element_type=F32)
    merged = (_sigmoid(gr_ref[...].astype(F32)) * br
              + _sigmoid(ga_ref[...].astype(F32)) * ba).astype(BF16)
    x1 = x_ref[...] + jnp.dot(merged, wo_ref[...], preferred_element_type=F32)
    h2 = _rms(x1, g2_ref[...], NORM_EPS).astype(BF16)
    x2 = x1
    for c in range(max(n_chunks, n_groups)):
        if c < n_groups:
            scan_group(c)
        if c < n_chunks:
            hid = jnp.dot(h2, w1_ref[:, c * d:(c + 1) * d], preferred_element_type=F32)
            hid = jnp.square(jnp.maximum(hid, 0.0)).astype(BF16)
            x2 = x2 + jnp.dot(hid, w2_ref[c * d:(c + 1) * d, :], preferred_element_type=F32)
    if final_norm:
        x2 = _rms(x2, gf_ref[...], NORM_EPS)
    o_ref[...] = x2


def _merge_mlp(x2, ya, gr, ga, a, u, ug, wr, wa, wo, g2, w1, w2, gf, seq, final_norm):
    t, d = x2.shape
    d_ff = w1.shape[1]
    tm = MLP_ROWS
    n = t // tm
    mlp_row = pl.BlockSpec((tm, d), lambda i: (jnp.maximum(i - 1, 0), 0))
    scan_row = pl.BlockSpec((tm, d), lambda i: (jnp.minimum(i, n - 1), 0))
    return pl.pallas_call(
        functools.partial(_mlp_kernel, d=d, d_ff=d_ff, n_row_tiles=n, per_seq=seq // tm,
                          final_norm=final_norm),
        grid=(n + 1,),
        in_specs=[mlp_row] * 4 + [scan_row] * 3 + [_whole()] * 7,
        out_specs=mlp_row,
        out_shape=jax.ShapeDtypeStruct((t, d), F32),
        scratch_shapes=[pltpu.VMEM((2, tm, d), BF16),
                        pltpu.VMEM((SUBLANES, d), F32)],
        compiler_params=pltpu.CompilerParams(
            dimension_semantics=("arbitrary",), vmem_limit_bytes=VMEM_LIMIT),
        name="merge_mlp",
    )(x2, ya, gr, ga, a, u, ug, wr, wa, wo, g2, w1, w2, gf)


def _rope_tables(seq):
    half = HEAD_DIM // 2
    inv_freq = ROPE_THETA ** (-jnp.arange(half, dtype=F32) * 2.0 / HEAD_DIM)
    ang = jnp.arange(seq, dtype=F32)[:, None] * inv_freq[None, :]
    cos, sin = jnp.cos(ang), jnp.sin(ang)
    reps = LANES // HEAD_DIM
    cos_t = jnp.tile(jnp.concatenate([cos, cos], axis=1), (1, reps))
    sin_t = jnp.tile(jnp.concatenate([-sin, sin], axis=1), (1, reps))
    return cos_t, sin_t


def _block_diag(w):
    n, bw, _ = w.shape
    per = MXU_DIM // bw
    w = w.reshape(n // per, per, bw, bw)
    eye = jnp.eye(per, dtype=w.dtype)
    return jnp.einsum('gpij,pq->gpiqj', w, eye).reshape(n // per, MXU_DIM, MXU_DIM)


def kernel(x, norm1_g, w_in, conv_w, conv_b, rg_a_w, rg_a_b, rg_x_w, rg_x_b, lru_lambda,
           lambda_q1, lambda_k1, lambda_q2, lambda_k2, subln_g, w_br_rnn, w_br_attn,
           w_out, norm2_g, w_mlp1, w_mlp2, normf_g):
    bsz, seq, d = x.shape
    depth = w_in.shape[0]
    t = bsz * seq
    cos_t, sin_t = _rope_tables(seq)
    xf = x.reshape(t, d)
    for l in range(depth):
        lam_init = 0.8 - 0.6 * math.exp(-0.3 * l)
        lru_a, lru_u, ug, qt, k, vt, gr, ga = _inproj(
            xf, norm1_g[l].reshape(1, d), w_in[l].astype(BF16), cos_t, sin_t,
            conv_w[l], conv_b[l].reshape(1, d),
            _block_diag(rg_a_w[l]).astype(BF16), rg_a_b[l].reshape(1, d),
            _block_diag(rg_x_w[l]).astype(BF16), rg_x_b[l].reshape(1, d),
            lru_lambda[l].reshape(1, d), bsz, seq)
        lam_vecs = jnp.stack([lambda_q1[l], lambda_k1[l], lambda_q2[l], lambda_k2[l]])
        y_attn = _attention(
            lam_vecs.astype(F32), subln_g[l].reshape(1, LANES),
            qt, k.reshape(bsz, seq, d), vt, lam_init)
        xf = _merge_mlp(
            xf, y_attn.reshape(t, d), gr, ga, lru_a, lru_u, ug,
            w_br_rnn[l].astype(BF16), w_br_attn[l].astype(BF16), w_out[l].astype(BF16),
            norm2_g[l].reshape(1, d), w_mlp1[l].astype(BF16), w_mlp2[l].astype(BF16),
            normf_g.reshape(1, d), seq, final_norm=(l == depth - 1))
    return xf.reshape(bsz, seq, d)
```

```python
import functools
import math

import jax
import jax.numpy as jnp
from jax import lax
from jax.experimental import pallas as pl
from jax.experimental.pallas import tpu as pltpu

F32 = jnp.float32
BF16 = jnp.bfloat16

CHUNK = 64
CONV_W = 4
LRU_C = 8.0
HEAD_DIM = 64
ROPE_THETA = 10000.0
NORM_EPS = 1e-6
SUBLN_EPS = 1e-5
MASK_VALUE = -1e30
F32_TINY = 1.1754944e-38
LOG2_E = 1.4426950408889634

LANES = 128
SUBLANES = 8
MXU_DIM = 256
VMEM_LIMIT = 56 * 1024 * 1024

PROJ_ROWS = 512
ATT_Q = 256
ATT_HEADS = 2
MLP_ROWS = 512


def _rms(x, g, eps):
    ms = jnp.mean(x * x, axis=-1, keepdims=True)
    return x * lax.rsqrt(ms + eps) * g


def _sigmoid(x):
    return 0.5 * jnp.tanh(0.5 * x) + 0.5


def _whole(memory_space=pltpu.VMEM):
    return pl.BlockSpec(memory_space=memory_space)


def _rope_cols(y, cos, sin_signed, first_half):
    outs = []
    for c in range(y.shape[1] // LANES):
        blk = y[:, c * LANES:(c + 1) * LANES]
        rot = jnp.where(first_half,
                        pltpu.roll(blk, LANES - HEAD_DIM // 2, axis=1),
                        pltpu.roll(blk, HEAD_DIM // 2, axis=1))
        outs.append(blk * cos + rot * sin_signed)
    return jnp.concatenate(outs, axis=1)


def _store_head_tiles(ref, y):
    yt = y.T
    for hd in range(ref.shape[0]):
        for j in range(ref.shape[1]):
            ref[hd, j] = yt[hd * LANES:(hd + 1) * LANES,
                            j * ATT_Q:(j + 1) * ATT_Q].astype(BF16)


def _store_head_rows(ref, y):
    yt = y.T
    for hd in range(ref.shape[0]):
        ref[hd] = yt[hd * LANES:(hd + 1) * LANES, :].astype(BF16)


def _lru_inputs_group(g, ux, tail_ref, cw_ref, cb_ref, wa_ref, ba_ref, wx_ref, bx_ref, lam_ref):
    ts, w = ux.shape
    cs = slice(g * w, (g + 1) * w)
    groups = ts // SUBLANES
    cw = cw_ref[:, cs]
    x3 = jnp.concatenate([tail_ref[:, cs], ux], axis=0).reshape(groups + 1, SUBLANES, w)
    sub = lax.broadcasted_iota(jnp.int32, (groups, SUBLANES, w), 1)
    xr = None
    for s in range(CONV_W - 1, 0, -1):
        rolled = pltpu.roll(x3, s, axis=1)
        shifted = jnp.where(sub >= s, rolled[1:], rolled[:-1])
        term = shifted * cw[CONV_W - 1 - s:CONV_W - s, :]
        xr = term if xr is None else xr + term
    xr = xr + x3[1:] * cw[CONV_W - 1:CONV_W, :]
    xr = xr.reshape(ts, w) + cb_ref[:, cs]
    tail_ref[:, cs] = ux[ts - SUBLANES:ts, :]

    xb = xr.astype(BF16)
    r = _sigmoid(jnp.dot(xb, wa_ref[g], preferred_element_type=F32) + ba_ref[:, cs])
    i = _sigmoid(jnp.dot(xb, wx_ref[g], preferred_element_type=F32) + bx_ref[:, cs])

    log_a = (-LRU_C * jax.nn.softplus(-lam_ref[:, cs])) * r
    a = jnp.exp(log_a)
    y = -jnp.tanh(log_a) * (a * a + 1.0)
    u = y * lax.rsqrt(jnp.maximum(y, F32_TINY)) * i * xr
    return a, u


def _inproj_kernel(x_ref, g_ref, w_ref, cos_ref, sin_ref,
                   cw_ref, cb_ref, wa_ref, ba_ref, wx_ref, bx_ref, lam_ref,
                   a_ref, u_ref, ug_ref, qt_ref, k_ref, vt_ref, gr_ref, ga_ref, tail_ref,
                   *, d, per_seq):
    @pl.when(pl.program_id(0) % per_seq == 0)
    def _():
        tail_ref[...] = jnp.zeros((SUBLANES, d), F32)

    h = _rms(x_ref[...], g_ref[...], NORM_EPS).astype(BF16)

    def proj(seg):
        return jnp.dot(h, w_ref[:, seg * d:(seg + 1) * d], preferred_element_type=F32)

    def lru_inputs(g, ux):
        cs = slice(g * MXU_DIM, (g + 1) * MXU_DIM)
        a, u = _lru_inputs_group(
            g, ux, tail_ref, cw_ref, cb_ref, wa_ref, ba_ref, wx_ref, bx_ref, lam_ref)
        a_ref[:, cs] = a
        u_ref[:, cs] = u.astype(BF16)

    cos = cos_ref[...]
    sin_signed = sin_ref[...]
    lane = lax.broadcasted_iota(jnp.int32, cos.shape, 1)
    first_half = (lane % HEAD_DIM) < (HEAD_DIM // 2)

    def ux_group(g):
        cs = slice(g * MXU_DIM, (g + 1) * MXU_DIM)
        return jnp.dot(h, w_ref[:, cs], preferred_element_type=F32)

    ux0 = ux_group(0)
    ug_ref[...] = proj(1).astype(BF16)
    ux1 = ux_group(1)
    lru_inputs(0, ux0)
    _store_head_tiles(qt_ref, _rope_cols(proj(2), cos, sin_signed, first_half)
                      * (HEAD_DIM ** -0.5 * LOG2_E))
    ux2 = ux_group(2)
    lru_inputs(1, ux1)
    k_ref[...] = _rope_cols(proj(3), cos, sin_signed, first_half).astype(BF16)
    ux3 = ux_group(3)
    lru_inputs(2, ux2)
    _store_head_rows(vt_ref, proj(4))
    lru_inputs(3, ux3)
    gr_ref[...] = proj(5).astype(BF16)
    ga_ref[...] = proj(6).astype(BF16)


def _inproj(x2, g, w_bf, cos_t, sin_t, cw, cb, wa_bd, ba, wx_bd, bx, lam, batch, seq):
    t, d = x2.shape
    tm = PROJ_ROWS
    per_seq = seq // tm
    heads = d // LANES
    row = pl.BlockSpec((tm, d), lambda i: (i, 0))
    tab = pl.BlockSpec((tm, LANES), lambda i: (i % per_seq, 0))
    tiles = pl.BlockSpec((None, heads, tm // ATT_Q, LANES, ATT_Q),
                         lambda i: (i // per_seq, 0, i % per_seq, 0, 0))
    cols = pl.BlockSpec((None, heads, LANES, tm), lambda i: (i // per_seq, 0, 0, i % per_seq))
    f32o = jax.ShapeDtypeStruct((t, d), F32)
    bf16o = jax.ShapeDtypeStruct((t, d), BF16)
    tileo = jax.ShapeDtypeStruct((batch, heads, seq // ATT_Q, LANES, ATT_Q), BF16)
    colo = jax.ShapeDtypeStruct((batch, heads, LANES, seq), BF16)
    return pl.pallas_call(
        functools.partial(_inproj_kernel, d=d, per_seq=per_seq),
        grid=(t // tm,),
        in_specs=[row, _whole(), _whole(), tab, tab] + [_whole()] * 7,
        out_specs=[row, row, row, tiles, row, cols, row, row],
        out_shape=[f32o, bf16o, bf16o, tileo, bf16o, colo, bf16o, bf16o],
        scratch_shapes=[pltpu.VMEM((SUBLANES, d), F32)],
        compiler_params=pltpu.CompilerParams(
            dimension_semantics=("arbitrary",), vmem_limit_bytes=VMEM_LIMIT),
        name="inproj",
    )(x2, g, w_bf, cos_t, sin_t, cw, cb, wa_bd, ba, wx_bd, bx, lam)


def _lru_scan_group(g, ts, a_ref, u_ref, ug_ref, hc_ref):
    w = MXU_DIM
    cs = slice(g * w, (g + 1) * w)
    groups = ts // SUBLANES
    a3 = a_ref[:, cs].reshape(groups, SUBLANES, w)
    h3 = u_ref[:, cs].astype(F32).reshape(groups, SUBLANES, w)
    sub = lax.broadcasted_iota(jnp.int32, (groups, SUBLANES, w), 1)
    step = 1
    while step < SUBLANES:
        keep = sub >= step
        a_prev = jnp.where(keep, pltpu.roll(a3, step, axis=1), 1.0)
        h_prev = jnp.where(keep, pltpu.roll(h3, step, axis=1), 0.0)
        h3 = h3 + a3 * h_prev
        a3 = a3 * a_prev
        step *= 2

    carry = hc_ref[:, cs]
    outs = []
    for q in range(groups):
        hq = h3[q] + a3[q] * carry
        outs.append(hq)
        carry = jnp.broadcast_to(hq[SUBLANES - 1:SUBLANES, :], (SUBLANES, w))
    hc_ref[:, cs] = carry
    gate = jax.nn.gelu(ug_ref[:, cs].astype(F32))
    return (jnp.concatenate(outs, axis=0) * gate).astype(BF16)


def _attn_kernel(lam_ref, g_ref, qt_ref, k_ref, vt_ref, o_ref, s_ref, p_ref, *, lam_init):
    n_heads, n_tiles, feat, tq = qt_ref.shape
    lv = lam_ref[...]
    lam = (jnp.exp(jnp.sum(lv[0:1, :] * lv[1:2, :], axis=-1, keepdims=True))
           - jnp.exp(jnp.sum(lv[2:3, :] * lv[3:4, :], axis=-1, keepdims=True))
           + lam_init)
    first_sub = lax.broadcasted_iota(jnp.int32, (feat, tq), 0) < HEAD_DIM
    key_c = lax.broadcasted_iota(jnp.int32, (tq, 2 * tq), 0) // CHUNK
    qry_c = lax.broadcasted_iota(jnp.int32, (tq, 2 * tq), 1) % tq // CHUNK
    diag_ok = key_c <= qry_c

    def stacked_q(hd, i):
        qt = qt_ref[hd, i].astype(F32)
        return jnp.concatenate([jnp.where(first_sub, qt, 0.0),
                                jnp.where(first_sub, 0.0, qt)], axis=1).astype(BF16)

    def score_chunk(slot, hd, i, c, qq, mx):
        s = jnp.dot(k_ref[c * tq:(c + 1) * tq, hd * feat:(hd + 1) * feat], qq,
                    preferred_element_type=F32)
        if c == i:
            s = jnp.where(diag_ok, s, MASK_VALUE)
        s_ref[slot, c] = s
        part = jnp.max(s.reshape(tq // SUBLANES, SUBLANES, 2 * tq), axis=0)
        return part if mx is None else jnp.maximum(mx, part)

    def prob_chunk(slot, c, m, lsum):
        p = jnp.exp2(s_ref[slot, c] - m)
        p_ref[slot, c * tq:(c + 1) * tq, :] = p.astype(BF16)
        return lsum + jnp.sum(p.reshape(tq // SUBLANES, SUBLANES, 2 * tq), axis=0)

    def finish(slot, hd, i, lsum):
        k_end = (i + 1) * tq
        acc = jnp.dot(vt_ref[hd, :, 0:k_end], p_ref[slot, 0:k_end, :],
                      preferred_element_type=F32)
        acc = acc * (1.0 / jnp.sum(lsum, axis=0, keepdims=True))
        o = (acc[:, 0:tq] - lam * acc[:, tq:2 * tq]).T
        o = _rms(o, g_ref[...], SUBLN_EPS) * (1.0 - lam_init)
        o_ref[i * tq:(i + 1) * tq, hd * feat:(hd + 1) * feat] = o.astype(BF16)

    order = list(range(0, n_tiles, 2)) + list(range(n_tiles - 1 - n_tiles % 2, 0, -2))
    items = [(hd, i) for hd in range(n_heads) for i in order]
    hd, i = items[0]
    qq = stacked_q(hd, i)
    mx = None
    for c in range(i + 1):
        mx = score_chunk(0, hd, i, c, qq, mx)
    for j, (hd, i) in enumerate(items):
        slot = j % 2
        m = jnp.max(mx, axis=0, keepdims=True)
        lsum = jnp.zeros((SUBLANES, 2 * tq), F32)
        nxt = items[j + 1] if j + 1 < len(items) else None
        n_next = nxt[1] + 1 if nxt else 0
        mx = None
        if nxt:
            qq = stacked_q(*nxt)
        for c in range(max(i + 1, n_next)):
            if c < n_next:
                mx = score_chunk(1 - slot, nxt[0], nxt[1], c, qq, mx)
            if c <= i:
                lsum = prob_chunk(slot, c, m, lsum)
        finish(slot, hd, i, lsum)


def _attention(lam_vecs, subln_g, qt, k, vt, lam_init):
    b, s, w = k.shape
    heads, n_tiles = qt.shape[1], qt.shape[2]
    hps = ATT_HEADS
    tiles = pl.BlockSpec((None, hps, n_tiles, LANES, ATT_Q), lambda bi, hi: (bi, hi, 0, 0, 0))
    cols = pl.BlockSpec((None, hps, LANES, s), lambda bi, hi: (bi, hi, 0, 0))
    rows = pl.BlockSpec((None, s, hps * LANES), lambda bi, hi: (bi, 0, hi))
    return pl.pallas_call(
        functools.partial(_attn_kernel, lam_init=lam_init),
        grid=(b, heads // hps),
        in_specs=[_whole(), _whole(), tiles, rows, cols],
        out_specs=rows,
        out_shape=jax.ShapeDtypeStruct((b, s, w), BF16),
        scratch_shapes=[pltpu.VMEM((2, n_tiles, ATT_Q, 2 * ATT_Q), F32),
                        pltpu.VMEM((2, s, 2 * ATT_Q), BF16)],
        compiler_params=pltpu.CompilerParams(
            dimension_semantics=("arbitrary", "arbitrary"), vmem_limit_bytes=VMEM_LIMIT),
        name="diffattn",
    )(lam_vecs, subln_g, qt, k, vt)


def _mlp_kernel(x_ref, ya_ref, gr_ref, ga_ref, a_ref, u_ref, ug_ref,
                wr_ref, wa_ref, wo_ref, g2_ref, w1_ref, w2_ref, gf_ref,
                o_ref, y_ref, hc_ref, *, d, d_ff, n_row_tiles, per_seq, final_norm):
    i = pl.program_id(0)
    ts = x_ref.shape[0]
    scan_tile = jnp.minimum(i, n_row_tiles - 1)

    @pl.when(scan_tile % per_seq == 0)
    def _():
        hc_ref[...] = jnp.zeros((SUBLANES, d), F32)

    @pl.when(i == 0)
    def _():
        y_ref[1] = jnp.zeros(y_ref.shape[1:], BF16)

    def scan_group(g):
        y_ref[i % 2, :, g * MXU_DIM:(g + 1) * MXU_DIM] = _lru_scan_group(
            g, ts, a_ref, u_ref, ug_ref, hc_ref)

    n_groups = d // MXU_DIM
    n_chunks = d_ff // d
    br = jnp.dot(y_ref[(i + 1) % 2], wr_ref[...], preferred_element_type=F32)
    ba = jnp.dot(ya_ref[...], wa_ref[...], preferred_element_type=F32)
    merged = (_sigmoid(gr_ref[...].astype(F32)) * br
              + _sigmoid(ga_ref[...].astype(F32)) * ba).astype(BF16)
    x1 = x_ref[...] + jnp.dot(merged, wo_ref[...], preferred_element_type=F32)
    h2 = _rms(x1, g2_ref[...], NORM_EPS).astype(BF16)
    x2 = x1
    for c in range(max(n_chunks, n_groups)):
        if c < n_groups:
            scan_group(c)
        if c < n_chunks:
            hid = jnp.dot(h2, w1_ref[:, c * d:(c + 1) * d], preferred_element_type=F32)
            hid = jnp.square(jnp.maximum(hid, 0.0)).astype(BF16)
            x2 = x2 + jnp.dot(hid, w2_ref[c * d:(c + 1) * d, :], preferred_element_type=F32)
    if final_norm:
        x2 = _rms(x2, gf_ref[...], NORM_EPS)
    o_ref[...] = x2


def _merge_mlp(x2, ya, gr, ga, a, u, ug, wr, wa, wo, g2, w1, w2, gf, seq, final_norm):
    t, d = x2.shape
    d_ff = w1.shape[1]
    tm = MLP_ROWS
    n = t // tm
    mlp_row = pl.BlockSpec((tm, d), lambda i: (jnp.maximum(i - 1, 0), 0))
    scan_row = pl.BlockSpec((tm, d), lambda i: (jnp.minimum(i, n - 1), 0))
    return pl.pallas_call(
        functools.partial(_mlp_kernel, d=d, d_ff=d_ff, n_row_tiles=n, per_seq=seq // tm,
                          final_norm=final_norm),
        grid=(n + 1,),
        in_specs=[mlp_row] * 4 + [scan_row] * 3 + [_whole()] * 7,
        out_specs=mlp_row,
        out_shape=jax.ShapeDtypeStruct((t, d), F32),
        scratch_shapes=[pltpu.VMEM((2, tm, d), BF16),
                        pltpu.VMEM((SUBLANES, d), F32)],
        compiler_params=pltpu.CompilerParams(
            dimension_semantics=("arbitrary",), vmem_limit_bytes=VMEM_LIMIT),
        name="merge_mlp",
    )(x2, ya, gr, ga, a, u, ug, wr, wa, wo, g2, w1, w2, gf)


def _rope_tables(seq):
    half = HEAD_DIM // 2
    inv_freq = ROPE_THETA ** (-jnp.arange(half, dtype=F32) * 2.0 / HEAD_DIM)
    ang = jnp.arange(seq, dtype=F32)[:, None] * inv_freq[None, :]
    cos, sin = jnp.cos(ang), jnp.sin(ang)
    reps = LANES // HEAD_DIM
    cos_t = jnp.tile(jnp.concatenate([cos, cos], axis=1), (1, reps))
    sin_t = jnp.tile(jnp.concatenate([-sin, sin], axis=1), (1, reps))
    return cos_t, sin_t


def _block_diag(w):
    n, bw, _ = w.shape
    per = MXU_DIM // bw
    w = w.reshape(n // per, per, bw, bw)
    eye = jnp.eye(per, dtype=w.dtype)
    return jnp.einsum('gpij,pq->gpiqj', w, eye).reshape(n // per, MXU_DIM, MXU_DIM)


def kernel(x, norm1_g, w_in, conv_w, conv_b, rg_a_w, rg_a_b, rg_x_w, rg_x_b, lru_lambda,
           lambda_q1, lambda_k1, lambda_q2, lambda_k2, subln_g, w_br_rnn, w_br_attn,
           w_out, norm2_g, w_mlp1, w_mlp2, normf_g):
    bsz, seq, d = x.shape
    depth = w_in.shape[0]
    t = bsz * seq
    cos_t, sin_t = _rope_tables(seq)
    xf = x.reshape(t, d)
    for l in range(depth):
        lam_init = 0.8 - 0.6 * math.exp(-0.3 * l)
        lru_a, lru_u, ug, qt, k, vt, gr, ga = _inproj(
            xf, norm1_g[l].reshape(1, d), w_in[l].astype(BF16), cos_t, sin_t,
            conv_w[l], conv_b[l].reshape(1, d),
            _block_diag(rg_a_w[l]).astype(BF16), rg_a_b[l].reshape(1, d),
            _block_diag(rg_x_w[l]).astype(BF16), rg_x_b[l].reshape(1, d),
            lru_lambda[l].reshape(1, d), bsz, seq)
        lam_vecs = jnp.stack([lambda_q1[l], lambda_k1[l], lambda_q2[l], lambda_k2[l]])
        y_attn = _attention(
            lam_vecs.astype(F32), subln_g[l].reshape(1, LANES),
            qt, k.reshape(bsz, seq, d), vt, lam_init)
        xf = _merge_mlp(
            xf, y_attn.reshape(t, d), gr, ga, lru_a, lru_u, ug,
            w_br_rnn[l].astype(BF16), w_br_attn[l].astype(BF16), w_out[l].astype(BF16),
            norm2_g[l].reshape(1, d), w_mlp1[l].astype(BF16), w_mlp2[l].astype(BF16),
            normf_g.reshape(1, d), seq, final_norm=(l == depth - 1))
    return xf.reshape(bsz, seq, d)
```

```python
import functools
import math

import jax
import jax.numpy as jnp
from jax import lax
from jax.experimental import pallas as pl
from jax.experimental.pallas import tpu as pltpu

F32 = jnp.float32
BF16 = jnp.bfloat16

CHUNK = 64
CONV_W = 4
LRU_C = 8.0
HEAD_DIM = 64
ROPE_THETA = 10000.0
NORM_EPS = 1e-6
SUBLN_EPS = 1e-5
MASK_VALUE = -1e30
F32_TINY = 1.1754944e-38
LOG2_E = 1.4426950408889634

LANES = 128
SUBLANES = 8
MXU_DIM = 256
VMEM_LIMIT = 56 * 1024 * 1024

PROJ_ROWS = 512
ATT_Q = 256
ATT_HEADS = 2
MLP_ROWS = 512


def _rms(x, g, eps):
    ms = jnp.mean(x * x, axis=-1, keepdims=True)
    return x * lax.rsqrt(ms + eps) * g


def _sigmoid(x):
    return 0.5 * jnp.tanh(0.5 * x) + 0.5


def _whole(memory_space=pltpu.VMEM):
    return pl.BlockSpec(memory_space=memory_space)


def _rope_cols(y, cos, sin_signed, first_half):
    outs = []
    for c in range(y.shape[1] // LANES):
        blk = y[:, c * LANES:(c + 1) * LANES]
        rot = jnp.where(first_half,
                        pltpu.roll(blk, LANES - HEAD_DIM // 2, axis=1),
                        pltpu.roll(blk, HEAD_DIM // 2, axis=1))
        outs.append(blk * cos + rot * sin_signed)
    return jnp.concatenate(outs, axis=1)


def _store_head_tiles(ref, y):
    yt = y.T
    for hd in range(ref.shape[0]):
        for j in range(ref.shape[1]):
            ref[hd, j] = yt[hd * LANES:(hd + 1) * LANES,
                            j * ATT_Q:(j + 1) * ATT_Q].astype(BF16)


def _store_head_rows(ref, y):
    yt = y.T
    for hd in range(ref.shape[0]):
        ref[hd] = yt[hd * LANES:(hd + 1) * LANES, :].astype(BF16)


def _lru_inputs_group(g, ux, tail_ref, cw_ref, cb_ref, wa_ref, ba_ref, wx_ref, bx_ref, lam_ref):
    ts, w = ux.shape
    cs = slice(g * w, (g + 1) * w)
    groups = ts // SUBLANES
    cw = cw_ref[:, cs]
    x3 = jnp.concatenate([tail_ref[:, cs], ux], axis=0).reshape(groups + 1, SUBLANES, w)
    sub = lax.broadcasted_iota(jnp.int32, (groups, SUBLANES, w), 1)
    xr = None
    for s in range(CONV_W - 1, 0, -1):
        rolled = pltpu.roll(x3, s, axis=1)
        shifted = jnp.where(sub >= s, rolled[1:], rolled[:-1])
        term = shifted * cw[CONV_W - 1 - s:CONV_W - s, :]
        xr = term if xr is None else xr + term
    xr = xr + x3[1:] * cw[CONV_W - 1:CONV_W, :]
    xr = xr.reshape(ts, w) + cb_ref[:, cs]
    tail_ref[:, cs] = ux[ts - SUBLANES:ts, :]

    xb = xr.astype(BF16)
    r = _sigmoid(jnp.dot(xb, wa_ref[g], preferred_element_type=F32) + ba_ref[:, cs])
    i = _sigmoid(jnp.dot(xb, wx_ref[g], preferred_element_type=F32) + bx_ref[:, cs])

    log_a = (-LRU_C * jax.nn.softplus(-lam_ref[:, cs])) * r
    a = jnp.exp(log_a)
    y = -jnp.tanh(log_a) * (a * a + 1.0)
    u = y * lax.rsqrt(jnp.maximum(y, F32_TINY)) * i * xr
    return a, u


def _lru_scan_group(g, a, u, ug, hc_ref):
    ts, w = a.shape
    cs = slice(g * w, (g + 1) * w)
    groups = ts // SUBLANES
    a3 = a.reshape(groups, SUBLANES, w)
    h3 = u.reshape(groups, SUBLANES, w)
    sub = lax.broadcasted_iota(jnp.int32, (groups, SUBLANES, w), 1)
    step = 1
    while step < SUBLANES:
        keep = sub >= step
        a_prev = jnp.where(keep, pltpu.roll(a3, step, axis=1), 1.0)
        h_prev = jnp.where(keep, pltpu.roll(h3, step, axis=1), 0.0)
        h3 = h3 + a3 * h_prev
        a3 = a3 * a_prev
        step *= 2

    carry = hc_ref[:, cs]
    outs = []
    for q in range(groups):
        hq = h3[q] + a3[q] * carry
        outs.append(hq)
        carry = jnp.broadcast_to(hq[SUBLANES - 1:SUBLANES, :], (SUBLANES, w))
    hc_ref[:, cs] = carry
    return (jnp.concatenate(outs, axis=0) * jax.nn.gelu(ug)).astype(BF16)


def _inproj_kernel(x_ref, g_ref, w_ref, cos_ref, sin_ref,
                   cw_ref, cb_ref, wa_ref, ba_ref, wx_ref, bx_ref, lam_ref,
                   y_ref, qt_ref, k_ref, vt_ref, gr_ref, ga_ref, tail_ref, hc_ref,
                   *, d, per_seq):
    @pl.when(pl.program_id(0) % per_seq == 0)
    def _():
        tail_ref[...] = jnp.zeros((SUBLANES, d), F32)
        hc_ref[...] = jnp.zeros((SUBLANES, d), F32)

    h = _rms(x_ref[...], g_ref[...], NORM_EPS).astype(BF16)

    def proj(seg):
        return jnp.dot(h, w_ref[:, seg * d:(seg + 1) * d], preferred_element_type=F32)

    def proj_group(seg, g):
        lo = seg * d + g * MXU_DIM
        return jnp.dot(h, w_ref[:, lo:lo + MXU_DIM], preferred_element_type=F32)

    def lru_inputs(g, ux):
        return _lru_inputs_group(
            g, ux, tail_ref, cw_ref, cb_ref, wa_ref, ba_ref, wx_ref, bx_ref, lam_ref)

    def lru_scan(g, a_u):
        cs = slice(g * MXU_DIM, (g + 1) * MXU_DIM)
        y_ref[:, cs] = _lru_scan_group(g, a_u[0], a_u[1], proj_group(1, g), hc_ref)

    cos = cos_ref[...]
    sin_signed = sin_ref[...]
    lane = lax.broadcasted_iota(jnp.int32, cos.shape, 1)
    first_half = (lane % HEAD_DIM) < (HEAD_DIM // 2)

    ux0 = proj_group(0, 0)
    ux1 = proj_group(0, 1)
    au0 = lru_inputs(0, ux0)
    _store_head_tiles(qt_ref, _rope_cols(proj(2), cos, sin_signed, first_half)
                      * (HEAD_DIM ** -0.5 * LOG2_E))
    ux2 = proj_group(0, 2)
    au1 = lru_inputs(1, ux1)
    k_ref[...] = _rope_cols(proj(3), cos, sin_signed, first_half).astype(BF16)
    ux3 = proj_group(0, 3)
    au2 = lru_inputs(2, ux2)
    _store_head_rows(vt_ref, proj(4))
    au3 = lru_inputs(3, ux3)
    gr_ref[...] = proj(5).astype(BF16)
    lru_scan(0, au0)
    lru_scan(1, au1)
    ga_ref[...] = proj(6).astype(BF16)
    lru_scan(2, au2)
    lru_scan(3, au3)


def _inproj(x2, g, w_bf, cos_t, sin_t, cw, cb, wa_bd, ba, wx_bd, bx, lam, batch, seq):
    t, d = x2.shape
    tm = PROJ_ROWS
    per_seq = seq // tm
    heads = d // LANES
    row = pl.BlockSpec((tm, d), lambda i: (i, 0))
    tab = pl.BlockSpec((tm, LANES), lambda i: (i % per_seq, 0))
    tiles = pl.BlockSpec((None, heads, tm // ATT_Q, LANES, ATT_Q),
                         lambda i: (i // per_seq, 0, i % per_seq, 0, 0))
    cols = pl.BlockSpec((None, heads, LANES, tm), lambda i: (i // per_seq, 0, 0, i % per_seq))
    bf16o = jax.ShapeDtypeStruct((t, d), BF16)
    tileo = jax.ShapeDtypeStruct((batch, heads, seq // ATT_Q, LANES, ATT_Q), BF16)
    colo = jax.ShapeDtypeStruct((batch, heads, LANES, seq), BF16)
    return pl.pallas_call(
        functools.partial(_inproj_kernel, d=d, per_seq=per_seq),
        grid=(t // tm,),
        in_specs=[row, _whole(), _whole(), tab, tab] + [_whole()] * 7,
        out_specs=[row, tiles, row, cols, row, row],
        out_shape=[bf16o, tileo, bf16o, colo, bf16o, bf16o],
        scratch_shapes=[pltpu.VMEM((SUBLANES, d), F32), pltpu.VMEM((SUBLANES, d), F32)],
        compiler_params=pltpu.CompilerParams(
            dimension_semantics=("arbitrary",), vmem_limit_bytes=VMEM_LIMIT),
        name="inproj",
    )(x2, g, w_bf, cos_t, sin_t, cw, cb, wa_bd, ba, wx_bd, bx, lam)


def _attn_kernel(lam_ref, g_ref, qt_ref, k_ref, vt_ref, o_ref, s_ref, p_ref, *, lam_init):
    n_heads, n_tiles, feat, tq = qt_ref.shape
    lv = lam_ref[...]
    lam = (jnp.exp(jnp.sum(lv[0:1, :] * lv[1:2, :], axis=-1, keepdims=True))
           - jnp.exp(jnp.sum(lv[2:3, :] * lv[3:4, :], axis=-1, keepdims=True))
           + lam_init)
    first_sub = lax.broadcasted_iota(jnp.int32, (feat, tq), 0) < HEAD_DIM
    key_c = lax.broadcasted_iota(jnp.int32, (tq, 2 * tq), 0) // CHUNK
    qry_c = lax.broadcasted_iota(jnp.int32, (tq, 2 * tq), 1) % tq // CHUNK
    diag_ok = key_c <= qry_c

    def stacked_q(hd, i):
        qt = qt_ref[hd, i].astype(F32)
        return jnp.concatenate([jnp.where(first_sub, qt, 0.0),
                                jnp.where(first_sub, 0.0, qt)], axis=1).astype(BF16)

    def score_chunk(slot, hd, i, c, qq, mx):
        s = jnp.dot(k_ref[c * tq:(c + 1) * tq, hd * feat:(hd + 1) * feat], qq,
                    preferred_element_type=F32)
        if c == i:
            s = jnp.where(diag_ok, s, MASK_VALUE)
        s_ref[slot, c] = s
        part = jnp.max(s.reshape(tq // SUBLANES, SUBLANES, 2 * tq), axis=0)
        return part if mx is None else jnp.maximum(mx, part)

    def prob_chunk(slot, c, m, lsum):
        p = jnp.exp2(s_ref[slot, c] - m)
        p_ref[slot, c * tq:(c + 1) * tq, :] = p.astype(BF16)
        return lsum + jnp.sum(p.reshape(tq // SUBLANES, SUBLANES, 2 * tq), axis=0)

    def finish(slot, hd, i, lsum):
        k_end = (i + 1) * tq
        acc = jnp.dot(vt_ref[hd, :, 0:k_end], p_ref[slot, 0:k_end, :],
                      preferred_element_type=F32)
        acc = acc * (1.0 / jnp.sum(lsum, axis=0, keepdims=True))
        o = (acc[:, 0:tq] - lam * acc[:, tq:2 * tq]).T
        o = _rms(o, g_ref[...], SUBLN_EPS) * (1.0 - lam_init)
        o_ref[i * tq:(i + 1) * tq, hd * feat:(hd + 1) * feat] = o.astype(BF16)

    order = list(range(0, n_tiles, 2)) + list(range(n_tiles - 1 - n_tiles % 2, 0, -2))
    items = [(hd, i) for hd in range(n_heads) for i in order]
    hd, i = items[0]
    qq = stacked_q(hd, i)
    mx = None
    for c in range(i + 1):
        mx = score_chunk(0, hd, i, c, qq, mx)
    for j, (hd, i) in enumerate(items):
        slot = j % 2
        m = jnp.max(mx, axis=0, keepdims=True)
        lsum = jnp.zeros((SUBLANES, 2 * tq), F32)
        nxt = items[j + 1] if j + 1 < len(items) else None
        n_next = nxt[1] + 1 if nxt else 0
        mx = None
        if nxt:
            qq = stacked_q(*nxt)
        for c in range(max(i + 1, n_next)):
            if c < n_next:
                mx = score_chunk(1 - slot, nxt[0], nxt[1], c, qq, mx)
            if c <= i:
                lsum = prob_chunk(slot, c, m, lsum)
        finish(slot, hd, i, lsum)


def _attention(lam_vecs, subln_g, qt, k, vt, lam_init):
    b, s, w = k.shape
    heads, n_tiles = qt.shape[1], qt.shape[2]
    hps = ATT_HEADS
    tiles = pl.BlockSpec((None, hps, n_tiles, LANES, ATT_Q), lambda bi, hi: (bi, hi, 0, 0, 0))
    cols = pl.BlockSpec((None, hps, LANES, s), lambda bi, hi: (bi, hi, 0, 0))
    rows = pl.BlockSpec((None, s, hps * LANES), lambda bi, hi: (bi, 0, hi))
    return pl.pallas_call(
        functools.partial(_attn_kernel, lam_init=lam_init),
        grid=(b, heads // hps),
        in_specs=[_whole(), _whole(), tiles, rows, cols],
        out_specs=rows,
        out_shape=jax.ShapeDtypeStruct((b, s, w), BF16),
        scratch_shapes=[pltpu.VMEM((2, n_tiles, ATT_Q, 2 * ATT_Q), F32),
                        pltpu.VMEM((2, s, 2 * ATT_Q), BF16)],
        compiler_params=pltpu.CompilerParams(
            dimension_semantics=("arbitrary", "arbitrary"), vmem_limit_bytes=VMEM_LIMIT),
        name="diffattn",
    )(lam_vecs, subln_g, qt, k, vt)


def _mlp_kernel(x_ref, yr_ref, ya_ref, gr_ref, ga_ref,
                wr_ref, wa_ref, wo_ref, g2_ref, w1_ref, w2_ref, gf_ref,
                o_ref, *, d, d_ff, final_norm):
    br = jnp.dot(yr_ref[...], wr_ref[...], preferred_element_type=F32)
    ba = jnp.dot(ya_ref[...], wa_ref[...], preferred_element_type=F32)
    merged = (_sigmoid(gr_ref[...].astype(F32)) * br
              + _sigmoid(ga_ref[...].astype(F32)) * ba).astype(BF16)
    x1 = x_ref[...] + jnp.dot(merged, wo_ref[...], preferred_element_type=F32)
    h2 = _rms(x1, g2_ref[...], NORM_EPS).astype(BF16)
    x2 = x1
    for c in range(d_ff // d):
        hid = jnp.dot(h2, w1_ref[:, c * d:(c + 1) * d], preferred_element_type=F32)
        hid = jnp.square(jnp.maximum(hid, 0.0)).astype(BF16)
        x2 = x2 + jnp.dot(hid, w2_ref[c * d:(c + 1) * d, :], preferred_element_type=F32)
    if final_norm:
        x2 = _rms(x2, gf_ref[...], NORM_EPS)
    o_ref[...] = x2


def _merge_mlp(x2, yr, ya, gr, ga, wr, wa, wo, g2, w1, w2, gf, final_norm):
    t, d = x2.shape
    d_ff = w1.shape[1]
    tm = MLP_ROWS
    row = pl.BlockSpec((tm, d), lambda i: (i, 0))
    return pl.pallas_call(
        functools.partial(_mlp_kernel, d=d, d_ff=d_ff, final_norm=final_norm),
        grid=(t // tm,),
        in_specs=[row] * 5 + [_whole()] * 7,
        out_specs=row,
        out_shape=jax.ShapeDtypeStruct((t, d), F32),
        compiler_params=pltpu.CompilerParams(
            dimension_semantics=("arbitrary",), vmem_limit_bytes=VMEM_LIMIT),
        name="merge_mlp",
    )(x2, yr, ya, gr, ga, wr, wa, wo, g2, w1, w2, gf)


def _rope_tables(seq):
    half = HEAD_DIM // 2
    inv_freq = ROPE_THETA ** (-jnp.arange(half, dtype=F32) * 2.0 / HEAD_DIM)
    ang = jnp.arange(seq, dtype=F32)[:, None] * inv_freq[None, :]
    cos, sin = jnp.cos(ang), jnp.sin(ang)
    reps = LANES // HEAD_DIM
    cos_t = jnp.tile(jnp.concatenate([cos, cos], axis=1), (1, reps))
    sin_t = jnp.tile(jnp.concatenate([-sin, sin], axis=1), (1, reps))
    return cos_t, sin_t


def _block_diag(w):
    n, bw, _ = w.shape
    per = MXU_DIM // bw
    w = w.reshape(n // per, per, bw, bw)
    eye = jnp.eye(per, dtype=w.dtype)
    return jnp.einsum('gpij,pq->gpiqj', w, eye).reshape(n // per, MXU_DIM, MXU_DIM)


def kernel(x, norm1_g, w_in, conv_w, conv_b, rg_a_w, rg_a_b, rg_x_w, rg_x_b, lru_lambda,
           lambda_q1, lambda_k1, lambda_q2, lambda_k2, subln_g, w_br_rnn, w_br_attn,
           w_out, norm2_g, w_mlp1, w_mlp2, normf_g):
    bsz, seq, d = x.shape
    depth = w_in.shape[0]
    t = bsz * seq
    cos_t, sin_t = _rope_tables(seq)
    xf = x.reshape(t, d)
    for l in range(depth):
        lam_init = 0.8 - 0.6 * math.exp(-0.3 * l)
        y_rnn, qt, k, vt, gr, ga = _inproj(
            xf, norm1_g[l].reshape(1, d), w_in[l].astype(BF16), cos_t, sin_t,
            conv_w[l], conv_b[l].reshape(1, d),
            _block_diag(rg_a_w[l]).astype(BF16), rg_a_b[l].reshape(1, d),
            _block_diag(rg_x_w[l]).astype(BF16), rg_x_b[l].reshape(1, d),
            lru_lambda[l].reshape(1, d), bsz, seq)
        lam_vecs = jnp.stack([lambda_q1[l], lambda_k1[l], lambda_q2[l], lambda_k2[l]])
        y_attn = _attention(
            lam_vecs.astype(F32), subln_g[l].reshape(1, LANES),
            qt, k.reshape(bsz, seq, d), vt, lam_init)
        xf = _merge_mlp(
            xf, y_rnn, y_attn.reshape(t, d), gr, ga,
            w_br_rnn[l].astype(BF16), w_br_attn[l].astype(BF16), w_out[l].astype(BF16),
            norm2_g[l].reshape(1, d), w_mlp1[l].astype(BF16), w_mlp2[l].astype(BF16),
            normf_g.reshape(1, d), final_norm=(l == depth - 1))
    return xf.reshape(bsz, seq, d)
```
